```python
import jax, jax.numpy as jnp
from jax import lax
import numpy as np

D_MODEL = 1024
BATCH = 32
SEQ = 2048
DEPTH = 1

FOX_HEADS = 8
FOX_HEAD_DIM = 64
FOX_WIDTH = FOX_HEADS * FOX_HEAD_DIM
MLA_HEADS = 8
MLA_NOPE_DIM = 64
MLA_ROPE_DIM = 32
MLA_QK_DIM = MLA_NOPE_DIM + MLA_ROPE_DIM
MLA_V_DIM = 64
MLA_Q_RANK = 256
MLA_KV_RANK = 128
MLA_WIDTH = MLA_HEADS * MLA_V_DIM
MIX_WIDTH = FOX_WIDTH + MLA_WIDTH
IN_SPLITS = (FOX_WIDTH, FOX_WIDTH, FOX_WIDTH, FOX_HEADS, MLA_Q_RANK, MLA_KV_RANK, MLA_ROPE_DIM)
IN_WIDTH = sum(IN_SPLITS)
D_FF = -(-8 * D_MODEL // (3 * 256)) * 256

Q_BLOCK = 128
ROPE_THETA = 10000.0
NORM_EPS = 1e-6

kernel_name = "hymba_fox_mla_hybrid_layer"


def rmsnorm(x, g):
    xf = x.astype(jnp.float32)
    y = xf * lax.rsqrt(jnp.mean(xf * xf, axis=-1, keepdims=True) + NORM_EPS)
    return (y * g.astype(jnp.float32)).astype(x.dtype)


def rope_tables(positions, dim):
    inv_freq = ROPE_THETA ** (-jnp.arange(0, dim, 2, dtype=jnp.float32) / dim)
    ang = positions.astype(jnp.float32)[:, :, None] * inv_freq[None, None, :]
    return jnp.cos(ang)[:, None], jnp.sin(ang)[:, None]


def apply_rope(x, cos, sin):
    xf = x.astype(jnp.float32)
    x1, x2 = jnp.split(xf, 2, axis=-1)
    return jnp.concatenate([x1 * cos - x2 * sin, x2 * cos + x1 * sin], axis=-1).astype(x.dtype)


def causal_block_attention(q, k, v, scale, log_decay=None):
    seq = q.shape[2]
    outs = []
    for blk in range(seq // Q_BLOCK):
        q0 = blk * Q_BLOCK
        q1 = q0 + Q_BLOCK
        qb = q[:, :, q0:q1]
        kb = k[:, :, :q1]
        vb = v[:, :, :q1]
        logits = jnp.einsum('bhqd,bhkd->bhqk', qb, kb,
                            preferred_element_type=jnp.float32) * scale
        if log_decay is not None:
            ld = log_decay.astype(jnp.float32)
            logits = logits + (ld[:, :, q0:q1, None] - ld[:, :, None, :q1])
        q_pos = q0 + jnp.arange(Q_BLOCK)
        k_pos = jnp.arange(q1)
        mask = k_pos[None, :] <= q_pos[:, None]
        logits = jnp.where(mask, logits, -jnp.inf)
        p = jax.nn.softmax(logits, axis=-1)
        outs.append(jnp.einsum('bhqk,bhkd->bhqd', p.astype(vb.dtype), vb))
    return jnp.concatenate(outs, axis=2)


def to_heads(t, n_heads):
    b, s, _ = t.shape
    return t.reshape(b, s, n_heads, -1).transpose(0, 2, 1, 3)


def from_heads(t):
    b, h, s, d = t.shape
    return t.transpose(0, 2, 1, 3).reshape(b, s, h * d)


def hybrid_mixer(h, cos, sin, w_in, b_fgate, q_norm_g, w_uq, kv_norm_g, w_ukv,
                 fox_out_g, mla_out_g, w_o):
    b, s, _ = h.shape
    proj = jnp.einsum('bsd,de->bse', h, w_in)
    cuts = list(np.cumsum(IN_SPLITS)[:-1])
    fq, fk, fv, f_logit, q_lat, kv_lat, k_rope = jnp.split(proj, cuts, axis=-1)

    log_f = jax.nn.log_sigmoid((f_logit + b_fgate).astype(jnp.float32))
    c = jnp.cumsum(log_f, axis=1).transpose(0, 2, 1)
    fox_o = causal_block_attention(to_heads(fq, FOX_HEADS), to_heads(fk, FOX_HEADS),
                                   to_heads(fv, FOX_HEADS), FOX_HEAD_DIM ** -0.5, c)
    fox_o = rmsnorm(from_heads(fox_o), fox_out_g)

    q = jnp.einsum('bsr,re->bse', rmsnorm(q_lat, q_norm_g), w_uq)
    q = to_heads(q, MLA_HEADS)
    q_nope, q_pe = q[..., :MLA_NOPE_DIM], q[..., MLA_NOPE_DIM:]
    q = jnp.concatenate([q_nope, apply_rope(q_pe, cos, sin)], axis=-1)
    kv = jnp.einsum('bsr,re->bse', rmsnorm(kv_lat, kv_norm_g), w_ukv)
    kv = to_heads(kv, MLA_HEADS)
    k_nope, v = kv[..., :MLA_NOPE_DIM], kv[..., MLA_NOPE_DIM:]
    k_pe = apply_rope(k_rope[:, None], cos, sin)
    k = jnp.concatenate([k_nope, jnp.broadcast_to(k_pe, (b, MLA_HEADS, s, MLA_ROPE_DIM))], axis=-1)
    mla_o = causal_block_attention(q, k, v, MLA_QK_DIM ** -0.5)
    mla_o = rmsnorm(from_heads(mla_o), mla_out_g)

    return jnp.einsum('bse,ed->bsd', jnp.concatenate([fox_o, mla_o], axis=-1), w_o)


def swiglu(h, w_gate, w_up, w_down):
    g = jnp.einsum('bsd,df->bsf', h, w_gate)
    u = jnp.einsum('bsd,df->bsf', h, w_up)
    return jnp.einsum('bsf,fd->bsd', jax.nn.silu(g) * u, w_down)


def setup_inputs(seed: int = 0) -> dict:
    key = jax.random.key(seed)
    ks = jax.random.split(key, 20)

    def w(k, shape, fan_in):
        return jax.random.normal(k, shape, jnp.float32) * fan_in ** -0.5

    def gain(k, shape):
        return 1.0 + 0.05 * jax.random.normal(k, shape, jnp.float32)

    x = jax.random.normal(ks[0], (BATCH, SEQ, D_MODEL), jnp.float32)
    offsets = jax.random.randint(ks[1], (BATCH, 1), 0, 4096, dtype=jnp.int32)
    positions = offsets + jnp.arange(SEQ, dtype=jnp.int32)[None, :]
    return {
        "x": x,
        "positions": positions,
        "norm_mix_g": gain(ks[2], (DEPTH, D_MODEL)),
        "w_in": w(ks[3], (DEPTH, D_MODEL, IN_WIDTH), D_MODEL),
        "b_fgate": 1.0 + 3.0 * jax.random.uniform(ks[4], (DEPTH, FOX_HEADS), jnp.float32),
        "q_norm_g": gain(ks[5], (DEPTH, MLA_Q_RANK)),
        "w_uq": w(ks[6], (DEPTH, MLA_Q_RANK, MLA_HEADS * MLA_QK_DIM), MLA_Q_RANK),
        "kv_norm_g": gain(ks[7], (DEPTH, MLA_KV_RANK)),
        "w_ukv": w(ks[8], (DEPTH, MLA_KV_RANK, MLA_HEADS * (MLA_NOPE_DIM + MLA_V_DIM)), MLA_KV_RANK),
        "fox_out_g": gain(ks[9], (DEPTH, FOX_WIDTH)),
        "mla_out_g": gain(ks[10], (DEPTH, MLA_WIDTH)),
        "w_o": w(ks[11], (DEPTH, MIX_WIDTH, D_MODEL), MIX_WIDTH),
        "norm_ffn_g": gain(ks[12], (DEPTH, D_MODEL)),
        "w_gate": w(ks[13], (DEPTH, D_MODEL, D_FF), D_MODEL),
        "w_up": w(ks[14], (DEPTH, D_MODEL, D_FF), D_MODEL),
        "w_down": w(ks[15], (DEPTH, D_FF, D_MODEL), D_FF),
        "final_norm_g": gain(ks[16], (D_MODEL,)),
    }


def reference(x, positions, norm_mix_g, w_in, b_fgate, q_norm_g, w_uq, kv_norm_g, w_ukv,
              fox_out_g, mla_out_g, w_o, norm_ffn_g, w_gate, w_up, w_down, final_norm_g):
    cos, sin = rope_tables(positions, MLA_ROPE_DIM)
    for l in range(DEPTH):
        h = rmsnorm(x, norm_mix_g[l])
        x = x + hybrid_mixer(h, cos, sin, w_in[l], b_fgate[l], q_norm_g[l], w_uq[l],
                             kv_norm_g[l], w_ukv[l], fox_out_g[l], mla_out_g[l], w_o[l])
        h = rmsnorm(x, norm_ffn_g[l])
        x = x + swiglu(h, w_gate[l], w_up[l], w_down[l])
    return rmsnorm(x, final_norm_g)
```

```python
import functools

import jax
import jax.numpy as jnp
from jax import lax
from jax.experimental import pallas as pl
from jax.experimental.pallas import tpu as pltpu

D_MODEL = 1024
FOX_HEADS = 8
FOX_HEAD_DIM = 64
FOX_WIDTH = FOX_HEADS * FOX_HEAD_DIM
MLA_HEADS = 8
MLA_NOPE_DIM = 64
MLA_ROPE_DIM = 32
MLA_QK_DIM = MLA_NOPE_DIM + MLA_ROPE_DIM
MLA_V_DIM = 64
MLA_Q_RANK = 256
MLA_KV_RANK = 128
MLA_WIDTH = MLA_HEADS * MLA_V_DIM
ROPE_THETA = 10000.0
NORM_EPS = 1e-6

LANES = 128
HALF = LANES // 2
PAIRS = FOX_HEADS // 2
PROJ_ROWS = 512
ATTN_TILE = 512
FFN_ROWS = 512
FFN_CHUNKS = 2
VMEM_LIMIT = 56 * 1024 * 1024
NEG_BIG = -1e30

F32 = jnp.float32
BF16 = jnp.bfloat16


def _rms(x, g):
    return x * lax.rsqrt(jnp.mean(x * x, axis=-1, keepdims=True) + NORM_EPS) * g


def _log_sigmoid(x):
    return jnp.minimum(x, 0.0) - jnp.log1p(jnp.exp(-jnp.abs(x)))


def _dot(a, b):
    return jnp.dot(a, b, preferred_element_type=F32)


def _proj_body(x_ref, pos_ref, gmix_ref, wa_ref, wb_ref, bf_ref, gq_ref, wuq_ref,
               gkv_ref, wuk_ref, wuv_ref, freq_ref, sign_ref,
               fq_ref, fk_ref, fv_ref, c_ref, mq_ref, mk_ref, mv_ref, carry_ref):
    ts = x_ref.shape[1]
    h = _rms(x_ref[0], gmix_ref[...]).astype(BF16)

    pa = _dot(h, wa_ref[...])
    for p in range(PAIRS):
        fq_ref[0, p] = (pa[:, p * LANES:(p + 1) * LANES] * (FOX_HEAD_DIM ** -0.5)).astype(BF16)
        fk_ref[0, p] = pa[:, FOX_WIDTH + p * LANES:FOX_WIDTH + (p + 1) * LANES].astype(BF16)
        fv_ref[0, p] = pa[:, 2 * FOX_WIDTH + p * LANES:2 * FOX_WIDTH + (p + 1) * LANES].astype(BF16)

    pb = _dot(h, wb_ref[...])
    q_lat = pb[:, :MLA_Q_RANK]
    kv_lat = pb[:, MLA_Q_RANK:MLA_Q_RANK + MLA_KV_RANK]
    misc = pb[:, MLA_Q_RANK + MLA_KV_RANK:]

    f_logit = misc.T[:FOX_HEADS, :] + bf_ref[...]
    log_f = _log_sigmoid(f_logit)
    hi = log_f.astype(BF16)
    r1 = log_f - hi.astype(F32)
    mid = r1.astype(BF16)
    lo = (r1 - mid.astype(F32)).astype(BF16)
    parts = jnp.concatenate([hi, mid, lo], axis=0)
    row = lax.broadcasted_iota(jnp.int32, (ts, ts), 0)
    col = lax.broadcasted_iota(jnp.int32, (ts, ts), 1)
    upper = jnp.where(row <= col, 1.0, 0.0).astype(BF16)
    cs = _dot(parts, upper)
    cs = cs[:FOX_HEADS] + cs[FOX_HEADS:2 * FOX_HEADS] + cs[2 * FOX_HEADS:]

    @pl.when(pl.program_id(1) == 0)
    def _():
        carry_ref[...] = jnp.zeros_like(carry_ref)

    c_tile = cs + carry_ref[...]
    c_ref[0] = c_tile
    carry_ref[...] = c_tile[:, ts - 1:ts]

    ang = pos_ref[0].astype(F32) * freq_ref[...]
    cos = jnp.cos(ang)
    sin_signed = jnp.sin(ang) * sign_ref[...]
    lane = lax.broadcasted_iota(jnp.int32, (1, LANES), 1)
    first_half = (lane >= MLA_NOPE_DIM) & (lane < MLA_NOPE_DIM + MLA_ROPE_DIM // 2)
    rope_lanes = (lane >= MLA_NOPE_DIM) & (lane < MLA_QK_DIM)

    def rope(t):
        swapped = jnp.where(first_half,
                            pltpu.roll(t, LANES - MLA_ROPE_DIM // 2, axis=1),
                            pltpu.roll(t, MLA_ROPE_DIM // 2, axis=1))
        return t * cos + swapped * sin_signed

    qn = _rms(q_lat, gq_ref[...]).astype(BF16)
    q_all = _dot(qn, wuq_ref[...])
    for hd in range(MLA_HEADS):
        qh = rope(q_all[:, hd * LANES:(hd + 1) * LANES])
        mq_ref[0, hd] = (qh * (MLA_QK_DIM ** -0.5)).astype(BF16)

    kvn = _rms(kv_lat, gkv_ref[...]).astype(BF16)
    k_all = _dot(kvn, wuk_ref[...])
    k_pe = rope(jnp.where(rope_lanes, misc, 0.0))
    for hd in range(MLA_HEADS):
        mk_ref[0, hd] = (k_all[:, hd * LANES:(hd + 1) * LANES] + k_pe).astype(BF16)
    v_all = _dot(kvn, wuv_ref[...])
    for p in range(PAIRS):
        mv_ref[0, p] = v_all[:, p * LANES:(p + 1) * LANES].astype(BF16)


def _full(shape):
    return pl.BlockSpec(shape, lambda *_: (0,) * len(shape))


def _projections(x, pos3, gmix, wa, wb, bf, gq, wuq, gkv, wuk, wuv, freq, sign):
    b, s, d = x.shape
    ts = PROJ_ROWS
    grid = (b, s // ts)
    pair_spec = pl.BlockSpec((1, PAIRS, ts, LANES), lambda i, j: (i, 0, j, 0))
    head_spec = pl.BlockSpec((1, MLA_HEADS, ts, LANES), lambda i, j: (i, 0, j, 0))
    pair_shape = jax.ShapeDtypeStruct((b, PAIRS, s, LANES), BF16)
    head_shape = jax.ShapeDtypeStruct((b, MLA_HEADS, s, LANES), BF16)
    return pl.pallas_call(
        _proj_body,
        grid=grid,
        in_specs=[
            pl.BlockSpec((1, ts, d), lambda i, j: (i, j, 0)),
            pl.BlockSpec((1, ts, 1), lambda i, j: (i, j, 0)),
            _full(gmix.shape), _full(wa.shape), _full(wb.shape), _full(bf.shape),
            _full(gq.shape), _full(wuq.shape), _full(gkv.shape), _full(wuk.shape),
            _full(wuv.shape), _full(freq.shape), _full(sign.shape),
        ],
        out_specs=[
            pair_spec, pair_spec, pair_spec,
            pl.BlockSpec((1, FOX_HEADS, ts), lambda i, j: (i, 0, j)),
            head_spec, head_spec, pair_spec,
        ],
        out_shape=[
            pair_shape, pair_shape, pair_shape,
            jax.ShapeDtypeStruct((b, FOX_HEADS, s), F32),
            head_shape, head_shape, pair_shape,
        ],
        scratch_shapes=[pltpu.VMEM((FOX_HEADS, 1), F32)],
        compiler_params=pltpu.CompilerParams(
            dimension_semantics=("parallel", "arbitrary"),
            vmem_limit_bytes=VMEM_LIMIT),
        name="proj",
    )(x, pos3, gmix, wa, wb, bf, gq, wuq, gkv, wuk, wuv, freq, sign)


def _softmax_tile(q, k, v, bias, mask, state):
    m, l, acc = state
    s = lax.dot_general(q, k, (((1,), (1,)), ((), ())), preferred_element_type=F32)
    if bias is not None:
        s = s + bias
    if mask is not None:
        s = jnp.where(mask, s, NEG_BIG)
    m_new = jnp.maximum(m, jnp.max(s, axis=-1, keepdims=True))
    alpha = jnp.exp(m - m_new)
    p = jnp.exp(s - m_new)
    l = alpha * l + jnp.sum(p, axis=-1, keepdims=True)
    acc = alpha * acc + _dot(p.astype(BF16), v)
    return m_new, l, acc


def _attend(get_q, get_k, get_v, get_bias, o_ref, seq):
    t = ATTN_TILE
    lane = lax.broadcasted_iota(jnp.int32, (1, LANES), 1)
    r = lax.broadcasted_iota(jnp.int32, (t, t), 0)
    c = lax.broadcasted_iota(jnp.int32, (t, t), 1)
    causal = c <= r
    for i in range(seq // t):
        q_rows = pl.ds(i * t, t)
        qs = [get_q(hh, q_rows) for hh in range(2)]

        def step(rows, mask, states, i=i, qs=qs):
            v = get_v(rows)
            return tuple(
                _softmax_tile(qs[hh], get_k(hh, rows), v,
                              None if get_bias is None else get_bias(hh, i, rows),
                              mask, states[hh])
                for hh in range(2))

        init = tuple((jnp.full((t, 1), NEG_BIG, F32), jnp.zeros((t, 1), F32),
                      jnp.zeros((t, LANES), F32)) for _ in range(2))
        if i > 0:
            states = lax.fori_loop(
                0, i,
                lambda j, st: step(pl.ds(pl.multiple_of(j * t, t), t), None, st),
                init)
        else:
            states = init
        states = step(q_rows, causal, states)
        outs = [acc / l for (_, l, acc) in states]
        o_ref[0, 0, q_rows, :] = jnp.where(lane < HALF, outs[0], outs[1]).astype(o_ref.dtype)


def _fox_attn_body(q_ref, k_ref, v_ref, c_ref, o_ref):
    seq = q_ref.shape[2]
    lane = lax.broadcasted_iota(jnp.int32, (1, LANES), 1)

    def get_q(hh, rows):
        sel = (lane < HALF) if hh == 0 else (lane >= HALF)
        return jnp.where(sel, q_ref[0, 0, rows, :], jnp.zeros((), BF16))

    def get_bias(hh, i, rows):
        c_start = c_ref[0, 0, hh:hh + 1, i * ATTN_TILE:i * ATTN_TILE + 1]
        return c_start - c_ref[0, 0, hh:hh + 1, rows]

    _attend(get_q, lambda hh, rows: k_ref[0, 0, rows, :],
            lambda rows: v_ref[0, 0, rows, :], get_bias, o_ref, seq)


def _mla_attn_body(q_ref, k_ref, v_ref, o_ref):
    seq = q_ref.shape[2]
    _attend(lambda hh, rows: q_ref[0, hh, rows, :],
            lambda hh, rows: k_ref[0, hh, rows, :],
            lambda rows: v_ref[0, 0, rows, :], None, o_ref, seq)


def _pair_block(s):
    return pl.BlockSpec((1, 1, s, LANES), lambda i, p: (i, p, 0, 0))


def _fox_attention(fq, fk, fv, c4):
    b, _, s, _ = fq.shape
    return pl.pallas_call(
        _fox_attn_body,
        grid=(b, PAIRS),
        in_specs=[_pair_block(s), _pair_block(s), _pair_block(s),
                  pl.BlockSpec((1, 1, 2, s), lambda i, p: (i, p, 0, 0))],
        out_specs=_pair_block(s),
        out_shape=jax.ShapeDtypeStruct((b, PAIRS, s, LANES), BF16),
        compiler_params=pltpu.CompilerParams(
            dimension_semantics=("parallel", "parallel"),
            vmem_limit_bytes=VMEM_LIMIT),
        name="fox_attn",
    )(fq, fk, fv, c4)


def _mla_attention(mq, mk, mv):
    b, _, s, _ = mq.shape
    two_heads = pl.BlockSpec((1, 2, s, LANES), lambda i, p: (i, p, 0, 0))
    return pl.pallas_call(
        _mla_attn_body,
        grid=(b, PAIRS),
        in_specs=[two_heads, two_heads, _pair_block(s)],
        out_specs=_pair_block(s),
        out_shape=jax.ShapeDtypeStruct((b, PAIRS, s, LANES), BF16),
        compiler_params=pltpu.CompilerParams(
            dimension_semantics=("parallel", "parallel"),
            vmem_limit_bytes=VMEM_LIMIT),
        name="mla_attn",
    )(mq, mk, mv)


def _out_body(x_ref, fo_ref, mo_ref, gfox_ref, gmla_ref, wo_ref, gffn_ref,
              wg_ref, wu_ref, wd_ref, gfin_ref, o_ref):
    fo = jnp.concatenate([fo_ref[0, p] for p in range(PAIRS)], axis=-1).astype(F32)
    mo = jnp.concatenate([mo_ref[0, p] for p in range(PAIRS)], axis=-1).astype(F32)
    mixed = jnp.concatenate([_rms(fo, gfox_ref[...]), _rms(mo, gmla_ref[...])],
                            axis=-1).astype(BF16)
    x1 = x_ref[0] + _dot(mixed, wo_ref[...])
    h = _rms(x1, gffn_ref[...]).astype(BF16)
    d_ff = wg_ref.shape[1]
    chunk = d_ff // FFN_CHUNKS
    z = None
    for ci in range(FFN_CHUNKS):
        cols = slice(ci * chunk, (ci + 1) * chunk)
        g = _dot(h, wg_ref[:, cols])
        u = _dot(h, wu_ref[:, cols])
        a = (g * (1.0 / (1.0 + jnp.exp(-g))) * u).astype(BF16)
        zc = _dot(a, wd_ref[cols, :])
        z = zc if z is None else z + zc
    o_ref[0] = _rms(x1 + z, gfin_ref[...])


def _const_spec(shape):
    return pl.BlockSpec(shape, lambda *_: (0,) * len(shape), pipeline_mode=pl.Buffered(1))


def _output_stage(x, fo, mo, gfox, gmla, wo, gffn, wg, wu, wd, gfin):
    b, s, d = x.shape
    ts = FFN_ROWS
    pair_spec = pl.BlockSpec((1, PAIRS, ts, LANES), lambda i, j: (i, 0, j, 0))
    tok_spec = pl.BlockSpec((1, ts, d), lambda i, j: (i, j, 0))
    return pl.pallas_call(
        _out_body,
        grid=(b, s // ts),
        in_specs=[tok_spec, pair_spec, pair_spec,
                  _const_spec(gfox.shape), _const_spec(gmla.shape), _const_spec(wo.shape),
                  _const_spec(gffn.shape), _const_spec(wg.shape), _const_spec(wu.shape),
                  _const_spec(wd.shape), _const_spec(gfin.shape)],
        out_specs=tok_spec,
        out_shape=jax.ShapeDtypeStruct((b, s, d), x.dtype),
        compiler_params=pltpu.CompilerParams(
            dimension_semantics=("parallel", "parallel"),
            vmem_limit_bytes=VMEM_LIMIT),
        name="out_ffn",
    )(x, fo, mo, gfox, gmla, wo, gffn, wg, wu, wd, gfin)


def _pad_heads(w, n_heads, width):
    k = w.shape[0]
    w = w.reshape(k, n_heads, width)
    w = jnp.pad(w, ((0, 0), (0, 0), (0, LANES - width)))
    return w.reshape(k, n_heads * LANES)


def kernel(x, positions, norm_mix_g, w_in, b_fgate, q_norm_g, w_uq, kv_norm_g, w_ukv,
           fox_out_g, mla_out_g, w_o, norm_ffn_g, w_gate, w_up, w_down, final_norm_g):
    b, s, d = x.shape
    assert norm_mix_g.shape[0] == 1, "single-layer block"
    assert s % ATTN_TILE == 0 and s % PROJ_ROWS == 0 and s % FFN_ROWS == 0

    w_in0 = w_in[0]
    o_f = 3 * FOX_WIDTH
    o_q = o_f + FOX_HEADS
    o_kv = o_q + MLA_Q_RANK
    o_kr = o_kv + MLA_KV_RANK
    wa = w_in0[:, :o_f].astype(BF16)
    misc = jnp.zeros((d, LANES), w_in0.dtype)
    misc = misc.at[:, :FOX_HEADS].set(w_in0[:, o_f:o_q])
    misc = misc.at[:, MLA_NOPE_DIM:MLA_QK_DIM].set(w_in0[:, o_kr:])
    wb = jnp.concatenate([w_in0[:, o_q:o_kr], misc], axis=1).astype(BF16)

    wuq = _pad_heads(w_uq[0], MLA_HEADS, MLA_QK_DIM).astype(BF16)
    w_ukv0 = w_ukv[0].reshape(MLA_KV_RANK, MLA_HEADS, MLA_NOPE_DIM + MLA_V_DIM)
    wuk = _pad_heads(w_ukv0[:, :, :MLA_NOPE_DIM].reshape(MLA_KV_RANK, -1),
                     MLA_HEADS, MLA_NOPE_DIM).astype(BF16)
    wuv = w_ukv0[:, :, MLA_NOPE_DIM:].reshape(MLA_KV_RANK, MLA_WIDTH).astype(BF16)

    half = MLA_ROPE_DIM // 2
    inv_freq = ROPE_THETA ** (-jnp.arange(0, MLA_ROPE_DIM, 2, dtype=F32) / MLA_ROPE_DIM)
    freq = jnp.zeros((1, LANES), F32)
    freq = freq.at[0, MLA_NOPE_DIM:MLA_NOPE_DIM + half].set(inv_freq)
    freq = freq.at[0, MLA_NOPE_DIM + half:MLA_QK_DIM].set(inv_freq)
    sign = jnp.zeros((1, LANES), F32)
    sign = sign.at[0, MLA_NOPE_DIM:MLA_NOPE_DIM + half].set(-1.0)
    sign = sign.at[0, MLA_NOPE_DIM + half:MLA_QK_DIM].set(1.0)

    fq, fk, fv, c, mq, mk, mv = _projections(
        x, positions.reshape(b, s, 1), norm_mix_g, wa, wb, b_fgate.reshape(FOX_HEADS, 1),
        q_norm_g, wuq, kv_norm_g, wuk, wuv, freq, sign)

    fo = _fox_attention(fq, fk, fv, c.reshape(b, PAIRS, 2, s))
    mo = _mla_attention(mq, mk, mv)

    return _output_stage(
        x, fo, mo, fox_out_g, mla_out_g, w_o[0].astype(BF16), norm_ffn_g,
        w_gate[0].astype(BF16), w_up[0].astype(BF16), w_down[0].astype(BF16),
        final_norm_g.reshape(1, d))
```

```python
import functools

import jax
import jax.numpy as jnp
from jax import lax
from jax.experimental import pallas as pl
from jax.experimental.pallas import tpu as pltpu

D_MODEL = 1024
FOX_HEADS = 8
FOX_HEAD_DIM = 64
FOX_WIDTH = FOX_HEADS * FOX_HEAD_DIM
MLA_HEADS = 8
MLA_NOPE_DIM = 64
MLA_ROPE_DIM = 32
MLA_QK_DIM = MLA_NOPE_DIM + MLA_ROPE_DIM
MLA_V_DIM = 64
MLA_Q_RANK = 256
MLA_KV_RANK = 128
MLA_WIDTH = MLA_HEADS * MLA_V_DIM
ROPE_THETA = 10000.0
NORM_EPS = 1e-6

LANES = 128
HALF = LANES // 2
PAIRS = FOX_HEADS // 2
PROJ_ROWS = 512
ATTN_TILE = 512
FFN_ROWS = 512
FFN_CHUNKS = 2
VMEM_LIMIT = 56 * 1024 * 1024
NEG_BIG = -1e30

F32 = jnp.float32
BF16 = jnp.bfloat16


def _rms(x, g):
    return x * lax.rsqrt(jnp.mean(x * x, axis=-1, keepdims=True) + NORM_EPS) * g


def _log_sigmoid(x):
    return jnp.minimum(x, 0.0) - jnp.log1p(jnp.exp(-jnp.abs(x)))


def _dot(a, b):
    return jnp.dot(a, b, preferred_element_type=F32)


def _proj_body(x_ref, pos_ref, gmix_ref, wa_ref, wb_ref, bf_ref, gq_ref, wuq_ref,
               gkv_ref, wuk_ref, wuv_ref, freq_ref, sign_ref,
               fq_ref, fk_ref, fv_ref, c_ref, mq_ref, mk_ref, mv_ref, carry_ref):
    ts = x_ref.shape[1]
    h = _rms(x_ref[0], gmix_ref[...]).astype(BF16)

    pa = _dot(h, wa_ref[...])
    for p in range(PAIRS):
        fq_ref[0, p] = (pa[:, p * LANES:(p + 1) * LANES] * (FOX_HEAD_DIM ** -0.5)).astype(BF16)
        fk_ref[0, p] = pa[:, FOX_WIDTH + p * LANES:FOX_WIDTH + (p + 1) * LANES].astype(BF16)
        fv_ref[0, p] = pa[:, 2 * FOX_WIDTH + p * LANES:2 * FOX_WIDTH + (p + 1) * LANES].astype(BF16)

    pb = _dot(h, wb_ref[...])
    q_lat = pb[:, :MLA_Q_RANK]
    kv_lat = pb[:, MLA_Q_RANK:MLA_Q_RANK + MLA_KV_RANK]
    misc = pb[:, MLA_Q_RANK + MLA_KV_RANK:]

    f_logit = misc.T[:FOX_HEADS, :] + bf_ref[...]
    log_f = _log_sigmoid(f_logit)
    hi = log_f.astype(BF16)
    r1 = log_f - hi.astype(F32)
    mid = r1.astype(BF16)
    lo = (r1 - mid.astype(F32)).astype(BF16)
    parts = jnp.concatenate([hi, mid, lo], axis=0)
    row = lax.broadcasted_iota(jnp.int32, (ts, ts), 0)
    col = lax.broadcasted_iota(jnp.int32, (ts, ts), 1)
    upper = jnp.where(row <= col, 1.0, 0.0).astype(BF16)
    cs = _dot(parts, upper)
    cs = cs[:FOX_HEADS] + cs[FOX_HEADS:2 * FOX_HEADS] + cs[2 * FOX_HEADS:]

    @pl.when(pl.program_id(1) == 0)
    def _():
        carry_ref[...] = jnp.zeros_like(carry_ref)

    c_tile = cs + carry_ref[...]
    c_ref[0] = c_tile
    carry_ref[...] = c_tile[:, ts - 1:ts]

    ang = pos_ref[0].astype(F32) * freq_ref[...]
    cos = jnp.cos(ang)
    sin_signed = jnp.sin(ang) * sign_ref[...]
    lane = lax.broadcasted_iota(jnp.int32, (1, LANES), 1)
    first_half = (lane >= MLA_NOPE_DIM) & (lane < MLA_NOPE_DIM + MLA_ROPE_DIM // 2)
    rope_lanes = (lane >= MLA_NOPE_DIM) & (lane < MLA_QK_DIM)

    def rope(t):
        swapped = jnp.where(first_half,
                            pltpu.roll(t, LANES - MLA_ROPE_DIM // 2, axis=1),
                            pltpu.roll(t, MLA_ROPE_DIM // 2, axis=1))
        return t * cos + swapped * sin_signed

    qn = _rms(q_lat, gq_ref[...]).astype(BF16)
    q_all = _dot(qn, wuq_ref[...])
    for hd in range(MLA_HEADS):
        qh = rope(q_all[:, hd * LANES:(hd + 1) * LANES])
        mq_ref[0, hd] = (qh * (MLA_QK_DIM ** -0.5)).astype(BF16)

    kvn = _rms(kv_lat, gkv_ref[...]).astype(BF16)
    k_all = _dot(kvn, wuk_ref[...])
    k_pe = rope(jnp.where(rope_lanes, misc, 0.0))
    for hd in range(MLA_HEADS):
        mk_ref[0, hd] = (k_all[:, hd * LANES:(hd + 1) * LANES] + k_pe).astype(BF16)
    v_all = _dot(kvn, wuv_ref[...])
    for p in range(PAIRS):
        mv_ref[0, p] = v_all[:, p * LANES:(p + 1) * LANES].astype(BF16)


def _full(shape):
    return pl.BlockSpec(shape, lambda *_: (0,) * len(shape))


def _projections(x, pos3, gmix, wa, wb, bf, gq, wuq, gkv, wuk, wuv, freq, sign):
    b, s, d = x.shape
    ts = PROJ_ROWS
    grid = (b, s // ts)
    pair_spec = pl.BlockSpec((1, PAIRS, ts, LANES), lambda i, j: (i, 0, j, 0))
    head_spec = pl.BlockSpec((1, MLA_HEADS, ts, LANES), lambda i, j: (i, 0, j, 0))
    pair_shape = jax.ShapeDtypeStruct((b, PAIRS, s, LANES), BF16)
    head_shape = jax.ShapeDtypeStruct((b, MLA_HEADS, s, LANES), BF16)
    return pl.pallas_call(
        _proj_body,
        grid=grid,
        in_specs=[
            pl.BlockSpec((1, ts, d), lambda i, j: (i, j, 0)),
            pl.BlockSpec((1, ts, 1), lambda i, j: (i, j, 0)),
            _full(gmix.shape), _full(wa.shape), _full(wb.shape), _full(bf.shape),
            _full(gq.shape), _full(wuq.shape), _full(gkv.shape), _full(wuk.shape),
            _full(wuv.shape), _full(freq.shape), _full(sign.shape),
        ],
        out_specs=[
            pair_spec, pair_spec, pair_spec,
            pl.BlockSpec((1, FOX_HEADS, ts), lambda i, j: (i, 0, j)),
            head_spec, head_spec, pair_spec,
        ],
        out_shape=[
            pair_shape, pair_shape, pair_shape,
            jax.ShapeDtypeStruct((b, FOX_HEADS, s), F32),
            head_shape, head_shape, pair_shape,
        ],
        scratch_shapes=[pltpu.VMEM((FOX_HEADS, 1), F32)],
        compiler_params=pltpu.CompilerParams(
            dimension_semantics=("parallel", "arbitrary"),
            vmem_limit_bytes=VMEM_LIMIT),
        name="proj",
    )(x, pos3, gmix, wa, wb, bf, gq, wuq, gkv, wuk, wuv, freq, sign)


def _attend(get_q, get_k, get_v, get_bias, o_ref, s_buf, seq):
    t = ATTN_TILE
    lane = lax.broadcasted_iota(jnp.int32, (1, LANES), 1)
    for i in range(seq // t):
        q_rows = pl.ds(i * t, t)
        qs = [get_q(hh, q_rows) for hh in range(2)]

        def scores(hh, j, i=i, qs=qs):
            rows = pl.ds(j * t, t)
            s = lax.dot_general(qs[hh], get_k(hh, rows), (((1,), (1,)), ((), ())),
                                preferred_element_type=F32)
            if get_bias is not None:
                s = s + get_bias(hh, i, rows)
            if j == i:
                r = lax.broadcasted_iota(jnp.int32, (t, t), 0)
                c = lax.broadcasted_iota(jnp.int32, (t, t), 1)
                s = jnp.where(c <= r, s, NEG_BIG)
            return s

        m = [jnp.full((t, 1), NEG_BIG, F32) for _ in range(2)]
        m_next = [None, None]
        l_lanes = [jnp.zeros((t, LANES), F32) for _ in range(2)]
        acc = [jnp.zeros((t, LANES), F32) for _ in range(2)]

        def park(j, m=m, m_next=m_next, scores=scores):
            for hh in range(2):
                s = scores(hh, j)
                s_buf[hh, j % 2] = s
                m_next[hh] = jnp.maximum(m[hh], jnp.max(s, axis=-1, keepdims=True))

        park(0)
        for j in range(i + 1):
            v = get_v(pl.ds(j * t, t))
            for hh in range(2):
                alpha = jnp.exp(m[hh] - m_next[hh])
                p = jnp.exp(s_buf[hh, j % 2] - m_next[hh])
                p_lanes = p[:, :LANES]
                for cv in range(1, t // LANES):
                    p_lanes = p_lanes + p[:, cv * LANES:(cv + 1) * LANES]
                l_lanes[hh] = alpha * l_lanes[hh] + p_lanes
                acc[hh] = alpha * acc[hh] + _dot(p.astype(BF16), v)
                m[hh] = m_next[hh]
            if j < i:
                park(j + 1)
        outs = [acc[hh] / jnp.sum(l_lanes[hh], axis=-1, keepdims=True) for hh in range(2)]
        o_ref[0, 0, q_rows, :] = jnp.where(lane < HALF, outs[0], outs[1]).astype(o_ref.dtype)


def _fox_attn_body(q_ref, k_ref, v_ref, c_ref, o_ref, s_buf):
    seq = q_ref.shape[2]
    lane = lax.broadcasted_iota(jnp.int32, (1, LANES), 1)

    def get_q(hh, rows):
        sel = (lane < HALF) if hh == 0 else (lane >= HALF)
        return jnp.where(sel, q_ref[0, 0, rows, :], jnp.zeros((), BF16))

    def get_bias(hh, i, rows):
        c_start = c_ref[0, 0, hh:hh + 1, i * ATTN_TILE:i * ATTN_TILE + 1]
        return c_start - c_ref[0, 0, hh:hh + 1, rows]

    _attend(get_q, lambda hh, rows: k_ref[0, 0, rows, :],
            lambda rows: v_ref[0, 0, rows, :], get_bias, o_ref, s_buf, seq)


def _mla_attn_body(q_ref, k_ref, v_ref, o_ref, s_buf):
    seq = q_ref.shape[2]
    _attend(lambda hh, rows: q_ref[0, hh, rows, :],
            lambda hh, rows: k_ref[0, hh, rows, :],
            lambda rows: v_ref[0, 0, rows, :], None, o_ref, s_buf, seq)


def _pair_block(s):
    return pl.BlockSpec((1, 1, s, LANES), lambda i, p: (i, p, 0, 0))


def _score_scratch():
    return pltpu.VMEM((2, 2, ATTN_TILE, ATTN_TILE), F32)


def _fox_attention(fq, fk, fv, c4):
    b, _, s, _ = fq.shape
    return pl.pallas_call(
        _fox_attn_body,
        grid=(b, PAIRS),
        in_specs=[_pair_block(s), _pair_block(s), _pair_block(s),
                  pl.BlockSpec((1, 1, 2, s), lambda i, p: (i, p, 0, 0))],
        out_specs=_pair_block(s),
        out_shape=jax.ShapeDtypeStruct((b, PAIRS, s, LANES), BF16),
        scratch_shapes=[_score_scratch()],
        compiler_params=pltpu.CompilerParams(
            dimension_semantics=("parallel", "parallel"),
            vmem_limit_bytes=VMEM_LIMIT),
        name="fox_attn",
    )(fq, fk, fv, c4)


def _mla_attention(mq, mk, mv):
    b, _, s, _ = mq.shape
    two_heads = pl.BlockSpec((1, 2, s, LANES), lambda i, p: (i, p, 0, 0))
    return pl.pallas_call(
        _mla_attn_body,
        grid=(b, PAIRS),
        in_specs=[two_heads, two_heads, _pair_block(s)],
        out_specs=_pair_block(s),
        out_shape=jax.ShapeDtypeStruct((b, PAIRS, s, LANES), BF16),
        scratch_shapes=[_score_scratch()],
        compiler_params=pltpu.CompilerParams(
            dimension_semantics=("parallel", "parallel"),
            vmem_limit_bytes=VMEM_LIMIT),
        name="mla_attn",
    )(mq, mk, mv)


def _out_body(x_ref, fo_ref, mo_ref, gfox_ref, gmla_ref, wo_ref, gffn_ref,
              wg_ref, wu_ref, wd_ref, gfin_ref, o_ref):
    fo = jnp.concatenate([fo_ref[0, p] for p in range(PAIRS)], axis=-1).astype(F32)
    mo = jnp.concatenate([mo_ref[0, p] for p in range(PAIRS)], axis=-1).astype(F32)
    mixed = jnp.concatenate([_rms(fo, gfox_ref[...]), _rms(mo, gmla_ref[...])],
                            axis=-1).astype(BF16)
    x1 = x_ref[0] + _dot(mixed, wo_ref[...])
    h = _rms(x1, gffn_ref[...]).astype(BF16)
    d_ff = wg_ref.shape[1]
    chunk = d_ff // FFN_CHUNKS
    z = None
    for ci in range(FFN_CHUNKS):
        cols = slice(ci * chunk, (ci + 1) * chunk)
        g = _dot(h, wg_ref[:, cols])
        u = _dot(h, wu_ref[:, cols])
        a = (g * (1.0 / (1.0 + jnp.exp(-g))) * u).astype(BF16)
        zc = _dot(a, wd_ref[cols, :])
        z = zc if z is None else z + zc
    o_ref[0] = _rms(x1 + z, gfin_ref[...])


def _const_spec(shape):
    return pl.BlockSpec(shape, lambda *_: (0,) * len(shape), pipeline_mode=pl.Buffered(1))


def _output_stage(x, fo, mo, gfox, gmla, wo, gffn, wg, wu, wd, gfin):
    b, s, d = x.shape
    ts = FFN_ROWS
    pair_spec = pl.BlockSpec((1, PAIRS, ts, LANES), lambda i, j: (i, 0, j, 0))
    tok_spec = pl.BlockSpec((1, ts, d), lambda i, j: (i, j, 0))
    return pl.pallas_call(
        _out_body,
        grid=(b, s // ts),
        in_specs=[tok_spec, pair_spec, pair_spec,
                  _const_spec(gfox.shape), _const_spec(gmla.shape), _const_spec(wo.shape),
                  _const_spec(gffn.shape), _const_spec(wg.shape), _const_spec(wu.shape),
                  _const_spec(wd.shape), _const_spec(gfin.shape)],
        out_specs=tok_spec,
        out_shape=jax.ShapeDtypeStruct((b, s, d), x.dtype),
        compiler_params=pltpu.CompilerParams(
            dimension_semantics=("parallel", "parallel"),
            vmem_limit_bytes=VMEM_LIMIT),
        name="out_ffn",
    )(x, fo, mo, gfox, gmla, wo, gffn, wg, wu, wd, gfin)


def _pad_heads(w, n_heads, width):
    k = w.shape[0]
    w = w.reshape(k, n_heads, width)
    w = jnp.pad(w, ((0, 0), (0, 0), (0, LANES - width)))
    return w.reshape(k, n_heads * LANES)


def kernel(x, positions, norm_mix_g, w_in, b_fgate, q_norm_g, w_uq, kv_norm_g, w_ukv,
           fox_out_g, mla_out_g, w_o, norm_ffn_g, w_gate, w_up, w_down, final_norm_g):
    b, s, d = x.shape
    assert norm_mix_g.shape[0] == 1, "single-layer block"
    assert s % ATTN_TILE == 0 and s % PROJ_ROWS == 0 and s % FFN_ROWS == 0

    w_in0 = w_in[0]
    o_f = 3 * FOX_WIDTH
    o_q = o_f + FOX_HEADS
    o_kv = o_q + MLA_Q_RANK
    o_kr = o_kv + MLA_KV_RANK
    wa = w_in0[:, :o_f].astype(BF16)
    misc = jnp.zeros((d, LANES), w_in0.dtype)
    misc = misc.at[:, :FOX_HEADS].set(w_in0[:, o_f:o_q])
    misc = misc.at[:, MLA_NOPE_DIM:MLA_QK_DIM].set(w_in0[:, o_kr:])
    wb = jnp.concatenate([w_in0[:, o_q:o_kr], misc], axis=1).astype(BF16)

    wuq = _pad_heads(w_uq[0], MLA_HEADS, MLA_QK_DIM).astype(BF16)
    w_ukv0 = w_ukv[0].reshape(MLA_KV_RANK, MLA_HEADS, MLA_NOPE_DIM + MLA_V_DIM)
    wuk = _pad_heads(w_ukv0[:, :, :MLA_NOPE_DIM].reshape(MLA_KV_RANK, -1),
                     MLA_HEADS, MLA_NOPE_DIM).astype(BF16)
    wuv = w_ukv0[:, :, MLA_NOPE_DIM:].reshape(MLA_KV_RANK, MLA_WIDTH).astype(BF16)

    half = MLA_ROPE_DIM // 2
    inv_freq = ROPE_THETA ** (-jnp.arange(0, MLA_ROPE_DIM, 2, dtype=F32) / MLA_ROPE_DIM)
    freq = jnp.zeros((1, LANES), F32)
    freq = freq.at[0, MLA_NOPE_DIM:MLA_NOPE_DIM + half].set(inv_freq)
    freq = freq.at[0, MLA_NOPE_DIM + half:MLA_QK_DIM].set(inv_freq)
    sign = jnp.zeros((1, LANES), F32)
    sign = sign.at[0, MLA_NOPE_DIM:MLA_NOPE_DIM + half].set(-1.0)
    sign = sign.at[0, MLA_NOPE_DIM + half:MLA_QK_DIM].set(1.0)

    fq, fk, fv, c, mq, mk, mv = _projections(
        x, positions.reshape(b, s, 1), norm_mix_g, wa, wb, b_fgate.reshape(FOX_HEADS, 1),
        q_norm_g, wuq, kv_norm_g, wuk, wuv, freq, sign)

    fo = _fox_attention(fq, fk, fv, c.reshape(b, PAIRS, 2, s))
    mo = _mla_attention(mq, mk, mv)

    return _output_stage(
        x, fo, mo, fox_out_g, mla_out_g, w_o[0].astype(BF16), norm_ffn_g,
        w_gate[0].astype(BF16), w_up[0].astype(BF16), w_down[0].astype(BF16),
        final_norm_g.reshape(1, d))
```

```python
import functools

import jax
import jax.numpy as jnp
from jax import lax
from jax.experimental import pallas as pl
from jax.experimental.pallas import tpu as pltpu

D_MODEL = 1024
FOX_HEADS = 8
FOX_HEAD_DIM = 64
FOX_WIDTH = FOX_HEADS * FOX_HEAD_DIM
MLA_HEADS = 8
MLA_NOPE_DIM = 64
MLA_ROPE_DIM = 32
MLA_QK_DIM = MLA_NOPE_DIM + MLA_ROPE_DIM
MLA_V_DIM = 64
MLA_Q_RANK = 256
MLA_KV_RANK = 128
MLA_WIDTH = MLA_HEADS * MLA_V_DIM
ROPE_THETA = 10000.0
NORM_EPS = 1e-6

LANES = 128
HALF = LANES // 2
PAIRS = FOX_HEADS // 2
PROJ_ROWS = 1024
PROJ_SUB = 512
ATTN_TILE = 512
FFN_ROWS = 512
MXU_TILE = 256
FFN_CHUNK_TILES = 6
VMEM_LIMIT = 56 * 1024 * 1024
NEG_BIG = -1e30
LOG2E = 1.4426950408889634

F32 = jnp.float32
BF16 = jnp.bfloat16


def _rms(x, g):
    return x * lax.rsqrt(jnp.mean(x * x, axis=-1, keepdims=True) + NORM_EPS) * g


def _log_sigmoid(x):
    return jnp.minimum(x, 0.0) - jnp.log1p(jnp.exp(-jnp.abs(x)))


def _dot(a, b):
    return jnp.dot(a, b, preferred_element_type=F32)


def _proj_body(x_ref, pos_ref, gmix_ref, wa_ref, wb_ref, bf_ref, gq_ref, wuq_ref,
               gkv_ref, wuk_ref, wuv_ref, freq_ref, sign_ref,
               fq_ref, fk_ref, fv_ref, c_ref, mq_ref, mk_ref, mv_ref, carry_ref):
    ts = x_ref.shape[1]
    sub = PROJ_SUB
    lane = lax.broadcasted_iota(jnp.int32, (1, LANES), 1)
    first_half = (lane >= MLA_NOPE_DIM) & (lane < MLA_NOPE_DIM + MLA_ROPE_DIM // 2)
    rope_lanes = (lane >= MLA_NOPE_DIM) & (lane < MLA_QK_DIM)
    ones_col = jnp.where(lane == HALF, 1.0, 0.0)
    row = lax.broadcasted_iota(jnp.int32, (sub, sub), 0)
    col = lax.broadcasted_iota(jnp.int32, (sub, sub), 1)
    upper = jnp.where(row <= col, 1.0, 0.0).astype(BF16)
    q_scale = (MLA_QK_DIM ** -0.5) * LOG2E
    pad_lo = MLA_NOPE_DIM
    pad_hi = LANES - MLA_QK_DIM

    @pl.when(pl.program_id(1) == 0)
    def _():
        carry_ref[...] = jnp.zeros_like(carry_ref)

    carry = [carry_ref[...]]

    def rope(t, c, s):
        swapped = jnp.where(first_half,
                            pltpu.roll(t, LANES - MLA_ROPE_DIM // 2, axis=1),
                            pltpu.roll(t, MLA_ROPE_DIM // 2, axis=1))
        return t * c + swapped * s

    def store_values(v_pairs, out_ref, rows):
        for p in range(PAIRS):
            tile = v_pairs[:, p * LANES:(p + 1) * LANES]
            out_ref[0, 2 * p, rows, :] = jnp.where(lane < HALF, tile, ones_col).astype(BF16)
            out_ref[0, 2 * p + 1, rows, :] = jnp.where(
                lane < HALF, pltpu.roll(tile, HALF, axis=1), ones_col).astype(BF16)

    def sub_tile(off):
        rows = pl.ds(off, sub)
        h = _rms(x_ref[0, rows, :], gmix_ref[...]).astype(BF16)
        pb = _dot(h, wb_ref[...])
        q_lat = pb[:, :MLA_Q_RANK]
        kv_lat = pb[:, MLA_Q_RANK:MLA_Q_RANK + MLA_KV_RANK]
        misc = pb[:, MLA_Q_RANK + MLA_KV_RANK:]
        yield

        v_fox = _dot(h, wa_ref[:, 2 * FOX_WIDTH:3 * FOX_WIDTH])
        f_logit = misc.T[:FOX_HEADS, :] + bf_ref[...]
        log_f = _log_sigmoid(f_logit)
        hi = log_f.astype(BF16)
        r1 = log_f - hi.astype(F32)
        mid = r1.astype(BF16)
        lo = (r1 - mid.astype(F32)).astype(BF16)
        parts = jnp.concatenate([hi, mid, lo], axis=0)
        cs = _dot(parts, upper)
        c_tile = (cs[:FOX_HEADS] + cs[FOX_HEADS:2 * FOX_HEADS] + cs[2 * FOX_HEADS:]) + carry[0]
        c_ref[0, :, rows] = c_tile
        carry[0] = c_tile[:, sub - 1:sub]
        ang = freq_ref[...] * pos_ref[0, :, rows].astype(F32)
        cos = jnp.concatenate([jnp.ones((pad_lo, sub), F32), jnp.cos(ang),
                               jnp.zeros((pad_hi, sub), F32)], axis=0).T
        sin_signed = jnp.concatenate([jnp.zeros((pad_lo, sub), F32),
                                      jnp.sin(ang) * sign_ref[...],
                                      jnp.zeros((pad_hi, sub), F32)], axis=0).T
        qn = _rms(q_lat, gq_ref[...]).astype(BF16)
        kvn = _rms(kv_lat, gkv_ref[...]).astype(BF16)
        yield

        q_all = _dot(qn, wuq_ref[...])
        k_all = _dot(kvn, wuk_ref[...])
        v_mla = _dot(kvn, wuv_ref[...])
        store_values(v_fox, fv_ref, rows)
        yield

        pq = _dot(h, wa_ref[:, :FOX_WIDTH])
        for p in range(PAIRS):
            fq_ref[0, p, rows, :] = (pq[:, p * LANES:(p + 1) * LANES]
                                     * ((FOX_HEAD_DIM ** -0.5) * LOG2E)).astype(BF16)
        cos_q = cos * q_scale
        sin_q = sin_signed * q_scale
        for hd in range(MLA_HEADS):
            mq_ref[0, hd, rows, :] = rope(q_all[:, hd * LANES:(hd + 1) * LANES],
                                          cos_q, sin_q).astype(BF16)
        k_pe = rope(jnp.where(rope_lanes, misc, 0.0), cos, sin_signed)
        for hd in range(MLA_HEADS):
            mk_ref[0, hd, rows, :] = (k_all[:, hd * LANES:(hd + 1) * LANES] + k_pe).astype(BF16)
        store_values(v_mla, mv_ref, rows)
        yield

        pk = _dot(h, wa_ref[:, FOX_WIDTH:2 * FOX_WIDTH])
        for p in range(PAIRS):
            fk_ref[0, p, rows, :] = pk[:, p * LANES:(p + 1) * LANES].astype(BF16)
        yield

    tiles = [sub_tile(off) for off in range(0, ts, sub)]
    assert len(tiles) == 2
    for which in (0, 1, 0, 0, 1, 1, 0, 1, 0, 1):
        next(tiles[which])
    carry_ref[...] = carry[0]


def _full(shape):
    return pl.BlockSpec(shape, lambda *_: (0,) * len(shape))


def _projections(x, pos3, gmix, wa, wb, bf, gq, wuq, gkv, wuk, wuv, freq, sign):
    b, s, d = x.shape
    ts = PROJ_ROWS
    grid = (b, s // ts)
    pair_spec = pl.BlockSpec((1, PAIRS, ts, LANES), lambda i, j: (i, 0, j, 0))
    head_spec = pl.BlockSpec((1, MLA_HEADS, ts, LANES), lambda i, j: (i, 0, j, 0))
    pair_shape = jax.ShapeDtypeStruct((b, PAIRS, s, LANES), BF16)
    head_shape = jax.ShapeDtypeStruct((b, MLA_HEADS, s, LANES), BF16)
    return pl.pallas_call(
        _proj_body,
        grid=grid,
        in_specs=[
            pl.BlockSpec((1, ts, d), lambda i, j: (i, j, 0)),
            pl.BlockSpec((1, 1, ts), lambda i, j: (i, 0, j)),
            _full(gmix.shape), _full(wa.shape), _full(wb.shape), _full(bf.shape),
            _full(gq.shape), _full(wuq.shape), _full(gkv.shape), _full(wuk.shape),
            _full(wuv.shape), _full(freq.shape), _full(sign.shape),
        ],
        out_specs=[
            pair_spec, pair_spec, head_spec,
            pl.BlockSpec((1, FOX_HEADS, ts), lambda i, j: (i, 0, j)),
            head_spec, head_spec, head_spec,
        ],
        out_shape=[
            pair_shape, pair_shape, head_shape,
            jax.ShapeDtypeStruct((b, FOX_HEADS, s), F32),
            head_shape, head_shape, head_shape,
        ],
        scratch_shapes=[pltpu.VMEM((FOX_HEADS, 1), F32)],
        compiler_params=pltpu.CompilerParams(
            dimension_semantics=("parallel", "arbitrary"),
            vmem_limit_bytes=VMEM_LIMIT),
        name="proj",
    )(x, pos3, gmix, wa, wb, bf, gq, wuq, gkv, wuk, wuv, freq, sign)


def _attend(get_q, get_k, get_v, get_bias, o_ref, s_buf, seq):
    t = ATTN_TILE
    lane = lax.broadcasted_iota(jnp.int32, (1, LANES), 1)
    for i in range(seq // t):
        q_rows = pl.ds(i * t, t)
        qs = [get_q(hh, q_rows) for hh in range(2)]

        def scores(hh, j, i=i, qs=qs):
            rows = pl.ds(j * t, t)
            s = lax.dot_general(qs[hh], get_k(hh, rows), (((1,), (1,)), ((), ())),
                                preferred_element_type=F32)
            if get_bias is not None:
                s = s + get_bias(hh, i, rows)
            if j == i:
                r = lax.broadcasted_iota(jnp.int32, (t, t), 0)
                c = lax.broadcasted_iota(jnp.int32, (t, t), 1)
                s = jnp.where(c <= r, s, NEG_BIG)
            return s

        m = [jnp.full((t, 1), NEG_BIG, F32) for _ in range(2)]
        m_next = [None, None]
        acc = [jnp.zeros((t, LANES), F32) for _ in range(2)]

        def park(j, m=m, m_next=m_next, scores=scores):
            for hh in range(2):
                s = scores(hh, j)
                s_buf[hh, j % 2] = s
                m_next[hh] = jnp.maximum(m[hh], jnp.max(s, axis=-1, keepdims=True))

        park(0)
        for j in range(i + 1):
            for hh in range(2):
                alpha = jnp.exp2(m[hh] - m_next[hh])
                p = jnp.exp2(s_buf[hh, j % 2] - m_next[hh])
                acc[hh] = alpha * acc[hh] + _dot(p.astype(BF16), get_v(hh, pl.ds(j * t, t)))
                m[hh] = m_next[hh]
            if j < i:
                park(j + 1)
        outs = [acc[hh] / acc[hh][:, HALF:HALF + 1] for hh in range(2)]
        o_ref[0, 0, q_rows, :] = jnp.where(lane < HALF, outs[0],
                                           pltpu.roll(outs[1], HALF, axis=1)).astype(o_ref.dtype)


def _fox_attn_body(q_ref, k_ref, v_ref, c_ref, o_ref, s_buf):
    seq = q_ref.shape[2]
    lane = lax.broadcasted_iota(jnp.int32, (1, LANES), 1)

    def get_q(hh, rows):
        sel = (lane < HALF) if hh == 0 else (lane >= HALF)
        return jnp.where(sel, q_ref[0, 0, rows, :], jnp.zeros((), BF16))

    def get_bias(hh, i, rows):
        c_start = c_ref[0, 0, hh:hh + 1, i * ATTN_TILE:i * ATTN_TILE + 1]
        return (c_start - c_ref[0, 0, hh:hh + 1, rows]) * LOG2E

    _attend(get_q, lambda hh, rows: k_ref[0, 0, rows, :],
            lambda hh, rows: v_ref[0, hh, rows, :], get_bias, o_ref, s_buf, seq)


def _mla_attn_body(q_ref, k_ref, v_ref, o_ref, s_buf):
    seq = q_ref.shape[2]
    _attend(lambda hh, rows: q_ref[0, hh, rows, :],
            lambda hh, rows: k_ref[0, hh, rows, :],
            lambda hh, rows: v_ref[0, hh, rows, :], None, o_ref, s_buf, seq)


def _pair_block(s):
    return pl.BlockSpec((1, 1, s, LANES), lambda i, p: (i, p, 0, 0))


def _two_heads_block(s):
    return pl.BlockSpec((1, 2, s, LANES), lambda i, p: (i, p, 0, 0))


def _score_scratch():
    return pltpu.VMEM((2, 2, ATTN_TILE, ATTN_TILE), F32)


def _fox_attention(fq, fk, fv, c4):
    b, _, s, _ = fq.shape
    return pl.pallas_call(
        _fox_attn_body,
        grid=(b, PAIRS),
        in_specs=[_pair_block(s), _pair_block(s), _two_heads_block(s),
                  pl.BlockSpec((1, 1, 2, s), lambda i, p: (i, p, 0, 0))],
        out_specs=_pair_block(s),
        out_shape=jax.ShapeDtypeStruct((b, PAIRS, s, LANES), BF16),
        scratch_shapes=[_score_scratch()],
        compiler_params=pltpu.CompilerParams(
            dimension_semantics=("parallel", "parallel"),
            vmem_limit_bytes=VMEM_LIMIT),
        name="fox_attn",
    )(fq, fk, fv, c4)


def _mla_attention(mq, mk, mv):
    b, _, s, _ = mq.shape
    two_heads = _two_heads_block(s)
    return pl.pallas_call(
        _mla_attn_body,
        grid=(b, PAIRS),
        in_specs=[two_heads, two_heads, two_heads],
        out_specs=_pair_block(s),
        out_shape=jax.ShapeDtypeStruct((b, PAIRS, s, LANES), BF16),
        scratch_shapes=[_score_scratch()],
        compiler_params=pltpu.CompilerParams(
            dimension_semantics=("parallel", "parallel"),
            vmem_limit_bytes=VMEM_LIMIT),
        name="mla_attn",
    )(mq, mk, mv)


def _out_body(x_ref, fo_ref, mo_ref, gfox_ref, gmla_ref, wo_ref, gffn_ref,
              wg_ref, wu_ref, wd_ref, gfin_ref, o_ref):
    fo = jnp.concatenate([fo_ref[0, p] for p in range(PAIRS)], axis=-1).astype(F32)
    mo = jnp.concatenate([mo_ref[0, p] for p in range(PAIRS)], axis=-1).astype(F32)
    mixed = jnp.concatenate([_rms(fo, gfox_ref[...]), _rms(mo, gmla_ref[...])],
                            axis=-1).astype(BF16)
    x1 = x_ref[0] + _dot(mixed, wo_ref[...])
    h = _rms(x1, gffn_ref[...]).astype(BF16)
    d_ff = wg_ref.shape[1]
    chunk = FFN_CHUNK_TILES * MXU_TILE
    z = None
    for c0 in range(0, d_ff, chunk):
        cols = slice(c0, min(c0 + chunk, d_ff))
        g = _dot(h, wg_ref[:, cols])
        u = _dot(h, wu_ref[:, cols])
        a = (g * (1.0 / (1.0 + jnp.exp(-g))) * u).astype(BF16)
        zc = _dot(a, wd_ref[cols, :])
        z = zc if z is None else z + zc
    o_ref[0] = _rms(x1 + z, gfin_ref[...])


def _const_spec(shape):
    return pl.BlockSpec(shape, lambda *_: (0,) * len(shape), pipeline_mode=pl.Buffered(1))


def _output_stage(x, fo, mo, gfox, gmla, wo, gffn, wg, wu, wd, gfin):
    b, s, d = x.shape
    ts = FFN_ROWS
    pair_spec = pl.BlockSpec((1, PAIRS, ts, LANES), lambda i, j: (i, 0, j, 0))
    tok_spec = pl.BlockSpec((1, ts, d), lambda i, j: (i, j, 0))
    return pl.pallas_call(
        _out_body,
        grid=(b, s // ts),
        in_specs=[tok_spec, pair_spec, pair_spec,
                  _const_spec(gfox.shape), _const_spec(gmla.shape), _const_spec(wo.shape),
                  _const_spec(gffn.shape), _const_spec(wg.shape), _const_spec(wu.shape),
                  _const_spec(wd.shape), _const_spec(gfin.shape)],
        out_specs=tok_spec,
        out_shape=jax.ShapeDtypeStruct((b, s, d), x.dtype),
        compiler_params=pltpu.CompilerParams(
            dimension_semantics=("parallel", "parallel"),
            vmem_limit_bytes=VMEM_LIMIT),
        name="out_ffn",
    )(x, fo, mo, gfox, gmla, wo, gffn, wg, wu, wd, gfin)


def _pad_heads(w, n_heads, width):
    k = w.shape[0]
    w = w.reshape(k, n_heads, width)
    w = jnp.pad(w, ((0, 0), (0, 0), (0, LANES - width)))
    return w.reshape(k, n_heads * LANES)


def kernel(x, positions, norm_mix_g, w_in, b_fgate, q_norm_g, w_uq, kv_norm_g, w_ukv,
           fox_out_g, mla_out_g, w_o, norm_ffn_g, w_gate, w_up, w_down, final_norm_g):
    b, s, d = x.shape
    assert norm_mix_g.shape[0] == 1, "single-layer block"
    assert s % ATTN_TILE == 0 and s % PROJ_ROWS == 0 and s % FFN_ROWS == 0

    w_in0 = w_in[0]
    o_f = 3 * FOX_WIDTH
    o_q = o_f + FOX_HEADS
    o_kv = o_q + MLA_Q_RANK
    o_kr = o_kv + MLA_KV_RANK
    wa = w_in0[:, :o_f].astype(BF16)
    misc = jnp.zeros((d, LANES), w_in0.dtype)
    misc = misc.at[:, :FOX_HEADS].set(w_in0[:, o_f:o_q])
    misc = misc.at[:, MLA_NOPE_DIM:MLA_QK_DIM].set(w_in0[:, o_kr:])
    wb = jnp.concatenate([w_in0[:, o_q:o_kr], misc], axis=1).astype(BF16)

    wuq = _pad_heads(w_uq[0], MLA_HEADS, MLA_QK_DIM).astype(BF16)
    w_ukv0 = w_ukv[0].reshape(MLA_KV_RANK, MLA_HEADS, MLA_NOPE_DIM + MLA_V_DIM)
    wuk = _pad_heads(w_ukv0[:, :, :MLA_NOPE_DIM].reshape(MLA_KV_RANK, -1),
                     MLA_HEADS, MLA_NOPE_DIM).astype(BF16)
    wuv = w_ukv0[:, :, MLA_NOPE_DIM:].reshape(MLA_KV_RANK, MLA_WIDTH).astype(BF16)

    half = MLA_ROPE_DIM // 2
    inv_freq = ROPE_THETA ** (-jnp.arange(0, MLA_ROPE_DIM, 2, dtype=F32) / MLA_ROPE_DIM)
    freq = jnp.concatenate([inv_freq, inv_freq]).reshape(MLA_ROPE_DIM, 1)
    sign = jnp.concatenate([-jnp.ones((half,), F32), jnp.ones((half,), F32)]).reshape(MLA_ROPE_DIM, 1)

    fq, fk, fv, c, mq, mk, mv = _projections(
        x, positions.reshape(b, 1, s), norm_mix_g, wa, wb, b_fgate.reshape(FOX_HEADS, 1),
        q_norm_g, wuq, kv_norm_g, wuk, wuv, freq, sign)

    fo = _fox_attention(fq, fk, fv, c.reshape(b, PAIRS, 2, s))
    mo = _mla_attention(mq, mk, mv)

    return _output_stage(
        x, fo, mo, fox_out_g, mla_out_g, w_o[0].astype(BF16), norm_ffn_g,
        w_gate[0].astype(BF16), w_up[0].astype(BF16), w_down[0].astype(BF16),
        final_norm_g.reshape(1, d))
```

```python
import functools

import jax
import jax.numpy as jnp
from jax import lax
from jax.experimental import pallas as pl
from jax.experimental.pallas import tpu as pltpu

D_MODEL = 1024
FOX_HEADS = 8
FOX_HEAD_DIM = 64
FOX_WIDTH = FOX_HEADS * FOX_HEAD_DIM
MLA_HEADS = 8
MLA_NOPE_DIM = 64
MLA_ROPE_DIM = 32
MLA_QK_DIM = MLA_NOPE_DIM + MLA_ROPE_DIM
MLA_V_DIM = 64
MLA_Q_RANK = 256
MLA_KV_RANK = 128
MLA_WIDTH = MLA_HEADS * MLA_V_DIM
ROPE_THETA = 10000.0
NORM_EPS = 1e-6

LANES = 128
HALF = LANES // 2
VT_ROWS = HALF + 16
PAIRS = FOX_HEADS // 2
PROJ_ROWS = 1024
PROJ_SUB = 512
ATTN_TILE = 512
FFN_ROWS = 512
MXU_TILE = 256
FFN_CHUNK_TILES = 6
VMEM_LIMIT = 56 * 1024 * 1024
NEG_BIG = -1e30
LOG2E = 1.4426950408889634

F32 = jnp.float32
BF16 = jnp.bfloat16


def _rms(x, g):
    return x * lax.rsqrt(jnp.mean(x * x, axis=-1, keepdims=True) + NORM_EPS) * g


def _log_sigmoid(x):
    return jnp.minimum(x, 0.0) - jnp.log1p(jnp.exp(-jnp.abs(x)))


def _dot(a, b):
    return jnp.dot(a, b, preferred_element_type=F32)


def _proj_body(x_ref, pos_ref, gmix_ref, wa_ref, wb_ref, bf_ref, gq_ref, wuq_ref,
               gkv_ref, wuk_ref, wuv_ref, freq_ref, sign_ref,
               fq_ref, fk_ref, fv_ref, c_ref, mq_ref, mk_ref, mv_ref, carry_ref):
    ts = x_ref.shape[1]
    sub = PROJ_SUB
    lane = lax.broadcasted_iota(jnp.int32, (1, LANES), 1)
    first_half = (lane >= MLA_NOPE_DIM) & (lane < MLA_NOPE_DIM + MLA_ROPE_DIM // 2)
    rope_lanes = (lane >= MLA_NOPE_DIM) & (lane < MLA_QK_DIM)
    row = lax.broadcasted_iota(jnp.int32, (sub, sub), 0)
    col = lax.broadcasted_iota(jnp.int32, (sub, sub), 1)
    upper = jnp.where(row <= col, 1.0, 0.0).astype(BF16)
    q_scale = (MLA_QK_DIM ** -0.5) * LOG2E
    pad_lo = MLA_NOPE_DIM
    pad_hi = LANES - MLA_QK_DIM

    @pl.when(pl.program_id(1) == 0)
    def _():
        carry_ref[...] = jnp.zeros_like(carry_ref)

    carry = [carry_ref[...]]

    def rope(t, c, s):
        swapped = jnp.where(first_half,
                            pltpu.roll(t, LANES - MLA_ROPE_DIM // 2, axis=1),
                            pltpu.roll(t, MLA_ROPE_DIM // 2, axis=1))
        return t * c + swapped * s

    tail_row = lax.broadcasted_iota(jnp.int32, (VT_ROWS - HALF, sub), 0)
    ones_row = jnp.where(tail_row == 0, 1.0, 0.0).astype(BF16)

    def store_values(v_all, out_ref, rows):
        vt = v_all.T.astype(BF16)
        for hd in range(FOX_HEADS):
            out_ref[0, hd, :HALF, rows] = vt[hd * HALF:(hd + 1) * HALF, :]
            out_ref[0, hd, HALF:, rows] = ones_row

    def sub_tile(off):
        rows = pl.ds(off, sub)
        h = _rms(x_ref[0, rows, :], gmix_ref[...]).astype(BF16)
        pb = _dot(h, wb_ref[...])
        q_lat = pb[:, :MLA_Q_RANK]
        kv_lat = pb[:, MLA_Q_RANK:MLA_Q_RANK + MLA_KV_RANK]
        misc = pb[:, MLA_Q_RANK + MLA_KV_RANK:]
        yield

        v_fox = _dot(h, wa_ref[:, 2 * FOX_WIDTH:3 * FOX_WIDTH])
        f_logit = misc.T[:FOX_HEADS, :] + bf_ref[...]
        log_f = _log_sigmoid(f_logit)
        hi = log_f.astype(BF16)
        r1 = log_f - hi.astype(F32)
        mid = r1.astype(BF16)
        lo = (r1 - mid.astype(F32)).astype(BF16)
        parts = jnp.concatenate([hi, mid, lo], axis=0)
        cs = _dot(parts, upper)
        c_tile = (cs[:FOX_HEADS] + cs[FOX_HEADS:2 * FOX_HEADS] + cs[2 * FOX_HEADS:]) + carry[0]
        c_ref[0, :, rows] = c_tile
        carry[0] = c_tile[:, sub - 1:sub]
        ang = freq_ref[...] * pos_ref[0, :, rows].astype(F32)
        cos = jnp.concatenate([jnp.ones((pad_lo, sub), F32), jnp.cos(ang),
                               jnp.zeros((pad_hi, sub), F32)], axis=0).T
        sin_signed = jnp.concatenate([jnp.zeros((pad_lo, sub), F32),
                                      jnp.sin(ang) * sign_ref[...],
                                      jnp.zeros((pad_hi, sub), F32)], axis=0).T
        qn = _rms(q_lat, gq_ref[...]).astype(BF16)
        kvn = _rms(kv_lat, gkv_ref[...]).astype(BF16)
        yield

        q_all = _dot(qn, wuq_ref[...])
        k_all = _dot(kvn, wuk_ref[...])
        v_mla = _dot(kvn, wuv_ref[...])
        store_values(v_fox, fv_ref, rows)
        yield

        pq = _dot(h, wa_ref[:, :FOX_WIDTH])
        for p in range(PAIRS):
            fq_ref[0, p, :, rows] = (pq[:, p * LANES:(p + 1) * LANES]
                                     * ((FOX_HEAD_DIM ** -0.5) * LOG2E)).T.astype(BF16)
        cos_q = cos * q_scale
        sin_q = sin_signed * q_scale
        for hd in range(MLA_HEADS):
            mq_ref[0, hd, :, rows] = rope(q_all[:, hd * LANES:(hd + 1) * LANES],
                                          cos_q, sin_q).T.astype(BF16)
        k_pe = rope(jnp.where(rope_lanes, misc, 0.0), cos, sin_signed)
        for hd in range(MLA_HEADS):
            mk_ref[0, hd, rows, :] = (k_all[:, hd * LANES:(hd + 1) * LANES] + k_pe).astype(BF16)
        store_values(v_mla, mv_ref, rows)
        yield

        pk = _dot(h, wa_ref[:, FOX_WIDTH:2 * FOX_WIDTH])
        for p in range(PAIRS):
            fk_ref[0, p, rows, :] = pk[:, p * LANES:(p + 1) * LANES].astype(BF16)
        yield

    tiles = [sub_tile(off) for off in range(0, ts, sub)]
    assert len(tiles) == 2
    for which in (0, 1, 0, 0, 1, 1, 0, 1, 0, 1):
        next(tiles[which])
    carry_ref[...] = carry[0]


def _full(shape):
    return pl.BlockSpec(shape, lambda *_: (0,) * len(shape))


def _projections(x, pos3, gmix, wa, wb, bf, gq, wuq, gkv, wuk, wuv, freq, sign):
    b, s, d = x.shape
    ts = PROJ_ROWS
    grid = (b, s // ts)
    pair_spec = pl.BlockSpec((1, PAIRS, ts, LANES), lambda i, j: (i, 0, j, 0))
    head_spec = pl.BlockSpec((1, MLA_HEADS, ts, LANES), lambda i, j: (i, 0, j, 0))
    pair_shape = jax.ShapeDtypeStruct((b, PAIRS, s, LANES), BF16)
    head_shape = jax.ShapeDtypeStruct((b, MLA_HEADS, s, LANES), BF16)

    def t_spec(n, rows):
        return pl.BlockSpec((1, n, rows, ts), lambda i, j: (i, 0, 0, j))

    def t_shape(n, rows):
        return jax.ShapeDtypeStruct((b, n, rows, s), BF16)

    return pl.pallas_call(
        _proj_body,
        grid=grid,
        in_specs=[
            pl.BlockSpec((1, ts, d), lambda i, j: (i, j, 0)),
            pl.BlockSpec((1, 1, ts), lambda i, j: (i, 0, j)),
            _full(gmix.shape), _full(wa.shape), _full(wb.shape), _full(bf.shape),
            _full(gq.shape), _full(wuq.shape), _full(gkv.shape), _full(wuk.shape),
            _full(wuv.shape), _full(freq.shape), _full(sign.shape),
        ],
        out_specs=[
            t_spec(PAIRS, LANES), pair_spec, t_spec(FOX_HEADS, VT_ROWS),
            pl.BlockSpec((1, FOX_HEADS, ts), lambda i, j: (i, 0, j)),
            t_spec(MLA_HEADS, LANES), head_spec, t_spec(MLA_HEADS, VT_ROWS),
        ],
        out_shape=[
            t_shape(PAIRS, LANES), pair_shape, t_shape(FOX_HEADS, VT_ROWS),
            jax.ShapeDtypeStruct((b, FOX_HEADS, s), F32),
            t_shape(MLA_HEADS, LANES), head_shape, t_shape(MLA_HEADS, VT_ROWS),
        ],
        scratch_shapes=[pltpu.VMEM((FOX_HEADS, 1), F32)],
        compiler_params=pltpu.CompilerParams(
            dimension_semantics=("parallel", "arbitrary"),
            vmem_limit_bytes=VMEM_LIMIT),
        name="proj",
    )(x, pos3, gmix, wa, wb, bf, gq, wuq, gkv, wuk, wuv, freq, sign)


def _attend(get_qt, get_k, get_vt, get_bias, o_ref, s_buf, seq):
    t = ATTN_TILE
    blk = MXU_TILE
    nb = t // blk
    key = lax.broadcasted_iota(jnp.int32, (blk, blk), 0)
    qry = lax.broadcasted_iota(jnp.int32, (blk, blk), 1)
    steps = [(i, j) for i in range(seq // t) for j in range(i + 1)]
    qts = {}
    acc = {}
    m_after = {}

    def visible(i, j, kb, qb):
        return j < i or kb <= qb

    def tile_offset(hh, i, j):
        return None if get_bias is None else get_bias(hh, i, j, 0)[1]

    def park(n):
        i, j = steps[n]
        if j == 0:
            qts[i] = [get_qt(hh, pl.ds(i * t, t)) for hh in range(2)]
        m_after[n] = [[None] * nb for _ in range(2)]
        for hh in range(2):
            for qb in range(nb):
                mx = jnp.full((1, blk), NEG_BIG, F32) if j == 0 else m_after[n - 1][hh][qb]
                for kb in range(nb):
                    if not visible(i, j, kb, qb):
                        continue
                    s = _dot(get_k(hh, pl.ds(j * t + kb * blk, blk)),
                             qts[i][hh][:, qb * blk:(qb + 1) * blk])
                    if get_bias is not None:
                        key_part, _ = get_bias(hh, i, j, kb)
                        s = s + jnp.concatenate([key_part] * (blk // LANES), axis=1)
                    if j == i and kb == qb:
                        s = jnp.where(key <= qry, s, NEG_BIG)
                    s_buf[hh, n % 2, kb * blk:(kb + 1) * blk, qb * blk:(qb + 1) * blk] = s
                    block_max = jnp.max(s, axis=0, keepdims=True)
                    if get_bias is not None:
                        block_max = block_max + tile_offset(hh, i, j)
                    mx = jnp.maximum(mx, block_max)
                m_after[n][hh][qb] = mx

    def process(n):
        i, j = steps[n]
        if j == 0:
            acc[i] = [[jnp.zeros((VT_ROWS, blk), F32) for _ in range(nb)] for _ in range(2)]
        for hh in range(2):
            for qb in range(nb):
                n_keys = blk * sum(visible(i, j, kb, qb) for kb in range(nb))
                m_new = m_after[n][hh][qb]
                m_tile = m_new if get_bias is None else m_new - tile_offset(hh, i, j)
                p = jnp.exp2(s_buf[hh, n % 2, :n_keys, qb * blk:(qb + 1) * blk] - m_tile)
                pv = _dot(get_vt(hh, pl.ds(j * t, n_keys)), p.astype(BF16))
                if j == 0:
                    acc[i][hh][qb] = pv
                else:
                    acc[i][hh][qb] = jnp.exp2(m_after[n - 1][hh][qb] - m_new) * acc[i][hh][qb] + pv
        if j == i:
            outs = [jnp.concatenate([a[:HALF] / a[HALF:HALF + 1] for a in acc[i][hh]], axis=1)
                    for hh in range(2)]
            o_ref[0, 0, pl.ds(i * t, t), :] = jnp.concatenate(outs, axis=0).T.astype(o_ref.dtype)
            del acc[i], qts[i]

    park(0)
    for n in range(len(steps)):
        if n + 1 < len(steps):
            park(n + 1)
        process(n)


def _fox_attn_body(qt_ref, k_ref, vt_ref, c_ref, o_ref, s_buf):
    seq = k_ref.shape[2]
    t = ATTN_TILE
    feat = lax.broadcasted_iota(jnp.int32, (LANES, 1), 0)

    def get_qt(hh, cols):
        sel = (feat < HALF) if hh == 0 else (feat >= HALF)
        return jnp.where(sel, qt_ref[0, 0, :, cols], jnp.zeros((), BF16))

    c2 = c_ref[0, 0]
    r8 = lax.broadcasted_iota(jnp.int32, (8, seq), 0)
    c8 = jnp.where(r8 == 0, c2[0:1, :], jnp.where(r8 == 1, c2[1:2, :], 0.0))
    c_cols = jnp.concatenate([c8, jnp.zeros((LANES - 8, seq), F32)], axis=0).T

    def tile_start(hh, tile):
        return c_ref[0, 0, hh:hh + 1, tile * t:tile * t + 1]

    key_bias = [[jnp.broadcast_to((tile_start(hh, j) - c_cols[j * t:(j + 1) * t, hh:hh + 1])
                                  * LOG2E, (t, LANES))
                 for j in range(seq // t)] for hh in range(2)]

    def get_bias(hh, i, j, kb):
        rows = slice(kb * MXU_TILE, (kb + 1) * MXU_TILE)
        return key_bias[hh][j][rows], (tile_start(hh, i) - tile_start(hh, j)) * LOG2E

    _attend(get_qt, lambda hh, rows: k_ref[0, 0, rows, :],
            lambda hh, cols: vt_ref[0, hh, :, cols], get_bias, o_ref, s_buf, seq)


def _mla_attn_body(qt_ref, k_ref, vt_ref, o_ref, s_buf):
    seq = k_ref.shape[2]
    _attend(lambda hh, cols: qt_ref[0, hh, :, cols],
            lambda hh, rows: k_ref[0, hh, rows, :],
            lambda hh, cols: vt_ref[0, hh, :, cols], None, o_ref, s_buf, seq)


def _pair_block(s):
    return pl.BlockSpec((1, 1, s, LANES), lambda i, p: (i, p, 0, 0))


def _two_heads_block(s):
    return pl.BlockSpec((1, 2, s, LANES), lambda i, p: (i, p, 0, 0))


def _t_block(n, rows, s):
    return pl.BlockSpec((1, n, rows, s), lambda i, p: (i, p, 0, 0))


def _score_scratch():
    return pltpu.VMEM((2, 2, ATTN_TILE, ATTN_TILE), F32)


def _fox_attention(fq, fk, fv, c4):
    b, _, s, _ = fk.shape
    return pl.pallas_call(
        _fox_attn_body,
        grid=(b, PAIRS),
        in_specs=[_t_block(1, LANES, s), _pair_block(s), _t_block(2, VT_ROWS, s),
                  pl.BlockSpec((1, 1, 2, s), lambda i, p: (i, p, 0, 0))],
        out_specs=_pair_block(s),
        out_shape=jax.ShapeDtypeStruct((b, PAIRS, s, LANES), BF16),
        scratch_shapes=[_score_scratch()],
        compiler_params=pltpu.CompilerParams(
            dimension_semantics=("parallel", "parallel"),
            vmem_limit_bytes=VMEM_LIMIT),
        name="fox_attn",
    )(fq, fk, fv, c4)


def _mla_attention(mq, mk, mv):
    b, _, s, _ = mk.shape
    return pl.pallas_call(
        _mla_attn_body,
        grid=(b, PAIRS),
        in_specs=[_t_block(2, LANES, s), _two_heads_block(s), _t_block(2, VT_ROWS, s)],
        out_specs=_pair_block(s),
        out_shape=jax.ShapeDtypeStruct((b, PAIRS, s, LANES), BF16),
        scratch_shapes=[_score_scratch()],
        compiler_params=pltpu.CompilerParams(
            dimension_semantics=("parallel", "parallel"),
            vmem_limit_bytes=VMEM_LIMIT),
        name="mla_attn",
    )(mq, mk, mv)


def _out_body(x_ref, fo_ref, mo_ref, gfox_ref, gmla_ref, wo_ref, gffn_ref,
              wg_ref, wu_ref, wd_ref, gfin_ref, o_ref):
    fo = jnp.concatenate([fo_ref[0, p] for p in range(PAIRS)], axis=-1).astype(F32)
    mo = jnp.concatenate([mo_ref[0, p] for p in range(PAIRS)], axis=-1).astype(F32)
    mixed = jnp.concatenate([_rms(fo, gfox_ref[...]), _rms(mo, gmla_ref[...])],
                            axis=-1).astype(BF16)
    x1 = x_ref[0] + _dot(mixed, wo_ref[...])
    h = _rms(x1, gffn_ref[...]).astype(BF16)
    d_ff = wg_ref.shape[1]
    chunk = FFN_CHUNK_TILES * MXU_TILE
    z = None
    for c0 in range(0, d_ff, chunk):
        cols = slice(c0, min(c0 + chunk, d_ff))
        g = _dot(h, wg_ref[:, cols])
        u = _dot(h, wu_ref[:, cols])
        a = (g * (1.0 / (1.0 + jnp.exp(-g))) * u).astype(BF16)
        zc = _dot(a, wd_ref[cols, :])
        z = zc if z is None else z + zc
    o_ref[0] = _rms(x1 + z, gfin_ref[...])


def _const_spec(shape):
    return pl.BlockSpec(shape, lambda *_: (0,) * len(shape), pipeline_mode=pl.Buffered(1))


def _output_stage(x, fo, mo, gfox, gmla, wo, gffn, wg, wu, wd, gfin):
    b, s, d = x.shape
    ts = FFN_ROWS
    pair_spec = pl.BlockSpec((1, PAIRS, ts, LANES), lambda i, j: (i, 0, j, 0))
    tok_spec = pl.BlockSpec((1, ts, d), lambda i, j: (i, j, 0))
    return pl.pallas_call(
        _out_body,
        grid=(b, s // ts),
        in_specs=[tok_spec, pair_spec, pair_spec,
                  _const_spec(gfox.shape), _const_spec(gmla.shape), _const_spec(wo.shape),
                  _const_spec(gffn.shape), _const_spec(wg.shape), _const_spec(wu.shape),
                  _const_spec(wd.shape), _const_spec(gfin.shape)],
        out_specs=tok_spec,
        out_shape=jax.ShapeDtypeStruct((b, s, d), x.dtype),
        compiler_params=pltpu.CompilerParams(
            dimension_semantics=("parallel", "parallel"),
            vmem_limit_bytes=VMEM_LIMIT),
        name="out_ffn",
    )(x, fo, mo, gfox, gmla, wo, gffn, wg, wu, wd, gfin)


def _pad_heads(w, n_heads, width):
    k = w.shape[0]
    w = w.reshape(k, n_heads, width)
    w = jnp.pad(w, ((0, 0), (0, 0), (0, LANES - width)))
    return w.reshape(k, n_heads * LANES)


def kernel(x, positions, norm_mix_g, w_in, b_fgate, q_norm_g, w_uq, kv_norm_g, w_ukv,
           fox_out_g, mla_out_g, w_o, norm_ffn_g, w_gate, w_up, w_down, final_norm_g):
    b, s, d = x.shape
    assert norm_mix_g.shape[0] == 1, "single-layer block"
    assert s % ATTN_TILE == 0 and s % PROJ_ROWS == 0 and s % FFN_ROWS == 0

    w_in0 = w_in[0]
    o_f = 3 * FOX_WIDTH
    o_q = o_f + FOX_HEADS
    o_kv = o_q + MLA_Q_RANK
    o_kr = o_kv + MLA_KV_RANK
    wa = w_in0[:, :o_f].astype(BF16)
    misc = jnp.zeros((d, LANES), w_in0.dtype)
    misc = misc.at[:, :FOX_HEADS].set(w_in0[:, o_f:o_q])
    misc = misc.at[:, MLA_NOPE_DIM:MLA_QK_DIM].set(w_in0[:, o_kr:])
    wb = jnp.concatenate([w_in0[:, o_q:o_kr], misc], axis=1).astype(BF16)

    wuq = _pad_heads(w_uq[0], MLA_HEADS, MLA_QK_DIM).astype(BF16)
    w_ukv0 = w_ukv[0].reshape(MLA_KV_RANK, MLA_HEADS, MLA_NOPE_DIM + MLA_V_DIM)
    wuk = _pad_heads(w_ukv0[:, :, :MLA_NOPE_DIM].reshape(MLA_KV_RANK, -1),
                     MLA_HEADS, MLA_NOPE_DIM).astype(BF16)
    wuv = w_ukv0[:, :, MLA_NOPE_DIM:].reshape(MLA_KV_RANK, MLA_WIDTH).astype(BF16)

    half = MLA_ROPE_DIM // 2
    inv_freq = ROPE_THETA ** (-jnp.arange(0, MLA_ROPE_DIM, 2, dtype=F32) / MLA_ROPE_DIM)
    freq = jnp.concatenate([inv_freq, inv_freq]).reshape(MLA_ROPE_DIM, 1)
    sign = jnp.concatenate([-jnp.ones((half,), F32), jnp.ones((half,), F32)]).reshape(MLA_ROPE_DIM, 1)

    fq, fk, fv, c, mq, mk, mv = _projections(
        x, positions.reshape(b, 1, s), norm_mix_g, wa, wb, b_fgate.reshape(FOX_HEADS, 1),
        q_norm_g, wuq, kv_norm_g, wuk, wuv, freq, sign)

    fo = _fox_attention(fq, fk, fv, c.reshape(b, PAIRS, 2, s))
    mo = _mla_attention(mq, mk, mv)

    return _output_stage(
        x, fo, mo, fox_out_g, mla_out_g, w_o[0].astype(BF16), norm_ffn_g,
        w_gate[0].astype(BF16), w_up[0].astype(BF16), w_down[0].astype(BF16),
        final_norm_g.reshape(1, d))
```

```python
import functools

import jax
import jax.numpy as jnp
from jax import lax
from jax.experimental import pallas as pl
from jax.experimental.pallas import tpu as pltpu

D_MODEL = 1024
FOX_HEADS = 8
FOX_HEAD_DIM = 64
FOX_WIDTH = FOX_HEADS * FOX_HEAD_DIM
MLA_HEADS = 8
MLA_NOPE_DIM = 64
MLA_ROPE_DIM = 32
MLA_QK_DIM = MLA_NOPE_DIM + MLA_ROPE_DIM
MLA_V_DIM = 64
MLA_Q_RANK = 256
MLA_KV_RANK = 128
MLA_WIDTH = MLA_HEADS * MLA_V_DIM
ROPE_THETA = 10000.0
NORM_EPS = 1e-6

LANES = 128
HALF = LANES // 2
VT_ROWS = HALF + 16
PAIRS = FOX_HEADS // 2
PROJ_ROWS = 1024
PROJ_SUB = 512
ATTN_TILE = 512
FFN_ROWS = 512
FFN_SUB = 256
MXU_TILE = 256
FFN_CHUNK_TILES = 6
VMEM_LIMIT = 56 * 1024 * 1024
NEG_BIG = -1e30
LOG2E = 1.4426950408889634

F32 = jnp.float32
BF16 = jnp.bfloat16


def _rms(x, g):
    return x * lax.rsqrt(jnp.mean(x * x, axis=-1, keepdims=True) + NORM_EPS) * g


def _log_sigmoid(x):
    return jnp.minimum(x, 0.0) - jnp.log1p(jnp.exp(-jnp.abs(x)))


def _dot(a, b):
    return jnp.dot(a, b, preferred_element_type=F32)


def _proj_body(x_ref, pos_ref, gmix_ref, wa_ref, wb_ref, bf_ref, gq_ref, wuq_ref,
               gkv_ref, wuk_ref, wuv_ref, freq_ref, sign_ref,
               fq_ref, fk_ref, fv_ref, c_ref, mq_ref, mk_ref, mv_ref, carry_ref):
    ts = x_ref.shape[1]
    sub = PROJ_SUB
    lane = lax.broadcasted_iota(jnp.int32, (1, LANES), 1)
    first_half = (lane >= MLA_NOPE_DIM) & (lane < MLA_NOPE_DIM + MLA_ROPE_DIM // 2)
    rope_lanes = (lane >= MLA_NOPE_DIM) & (lane < MLA_QK_DIM)
    row = lax.broadcasted_iota(jnp.int32, (sub, sub), 0)
    col = lax.broadcasted_iota(jnp.int32, (sub, sub), 1)
    upper = jnp.where(row <= col, 1.0, 0.0).astype(BF16)
    q_scale = (MLA_QK_DIM ** -0.5) * LOG2E
    pad_lo = MLA_NOPE_DIM
    pad_hi = LANES - MLA_QK_DIM

    @pl.when(pl.program_id(1) == 0)
    def _():
        carry_ref[...] = jnp.zeros_like(carry_ref)

    carry = [carry_ref[...]]

    def rope(t, c, s):
        swapped = jnp.where(first_half,
                            pltpu.roll(t, LANES - MLA_ROPE_DIM // 2, axis=1),
                            pltpu.roll(t, MLA_ROPE_DIM // 2, axis=1))
        return t * c + swapped * s

    tail_row = lax.broadcasted_iota(jnp.int32, (VT_ROWS - HALF, sub), 0)
    ones_row = jnp.where(tail_row == 0, 1.0, 0.0).astype(BF16)

    def store_values(v_all, out_ref, rows):
        vt = v_all.T.astype(BF16)
        for hd in range(FOX_HEADS):
            out_ref[0, hd, :HALF, rows] = vt[hd * HALF:(hd + 1) * HALF, :]
            out_ref[0, hd, HALF:, rows] = ones_row

    def sub_tile(off):
        rows = pl.ds(off, sub)
        h = _rms(x_ref[0, rows, :], gmix_ref[...]).astype(BF16)
        pb = _dot(h, wb_ref[...])
        q_lat = pb[:, :MLA_Q_RANK]
        kv_lat = pb[:, MLA_Q_RANK:MLA_Q_RANK + MLA_KV_RANK]
        misc = pb[:, MLA_Q_RANK + MLA_KV_RANK:]
        yield

        v_fox = _dot(h, wa_ref[:, 2 * FOX_WIDTH:3 * FOX_WIDTH])
        f_logit = misc.T[:FOX_HEADS, :] + bf_ref[...]
        log_f = _log_sigmoid(f_logit)
        hi = log_f.astype(BF16)
        r1 = log_f - hi.astype(F32)
        mid = r1.astype(BF16)
        lo = (r1 - mid.astype(F32)).astype(BF16)
        parts = jnp.concatenate([hi, mid, lo], axis=0)
        cs = _dot(parts, upper)
        c_tile = (cs[:FOX_HEADS] + cs[FOX_HEADS:2 * FOX_HEADS] + cs[2 * FOX_HEADS:]) + carry[0]
        c_ref[0, :, rows] = c_tile
        carry[0] = c_tile[:, sub - 1:sub]
        ang = freq_ref[...] * pos_ref[0, :, rows].astype(F32)
        cos = jnp.concatenate([jnp.ones((pad_lo, sub), F32), jnp.cos(ang),
                               jnp.zeros((pad_hi, sub), F32)], axis=0).T
        sin_signed = jnp.concatenate([jnp.zeros((pad_lo, sub), F32),
                                      jnp.sin(ang) * sign_ref[...],
                                      jnp.zeros((pad_hi, sub), F32)], axis=0).T
        qn = _rms(q_lat, gq_ref[...]).astype(BF16)
        kvn = _rms(kv_lat, gkv_ref[...]).astype(BF16)
        yield

        q_all = _dot(qn, wuq_ref[...])
        k_all = _dot(kvn, wuk_ref[...])
        v_mla = _dot(kvn, wuv_ref[...])
        store_values(v_fox, fv_ref, rows)
        yield

        pq = _dot(h, wa_ref[:, :FOX_WIDTH])
        for p in range(PAIRS):
            fq_ref[0, p, :, rows] = (pq[:, p * LANES:(p + 1) * LANES]
                                     * ((FOX_HEAD_DIM ** -0.5) * LOG2E)).T.astype(BF16)
        cos_q = cos * q_scale
        sin_q = sin_signed * q_scale
        for hd in range(MLA_HEADS):
            mq_ref[0, hd, :, rows] = rope(q_all[:, hd * LANES:(hd + 1) * LANES],
                                          cos_q, sin_q).T.astype(BF16)
        k_pe = rope(jnp.where(rope_lanes, misc, 0.0), cos, sin_signed)
        for hd in range(MLA_HEADS):
            mk_ref[0, hd, rows, :] = (k_all[:, hd * LANES:(hd + 1) * LANES] + k_pe).astype(BF16)
        store_values(v_mla, mv_ref, rows)
        yield

        pk = _dot(h, wa_ref[:, FOX_WIDTH:2 * FOX_WIDTH])
        for p in range(PAIRS):
            fk_ref[0, p, rows, :] = pk[:, p * LANES:(p + 1) * LANES].astype(BF16)
        yield

    tiles = [sub_tile(off) for off in range(0, ts, sub)]
    assert len(tiles) == 2
    for which in (0, 1, 0, 0, 1, 1, 0, 1, 0, 1):
        next(tiles[which])
    carry_ref[...] = carry[0]


def _full(shape):
    return pl.BlockSpec(shape, lambda *_: (0,) * len(shape))


def _projections(x, pos3, gmix, wa, wb, bf, gq, wuq, gkv, wuk, wuv, freq, sign):
    b, s, d = x.shape
    ts = PROJ_ROWS
    grid = (b, s // ts)
    pair_spec = pl.BlockSpec((1, PAIRS, ts, LANES), lambda i, j: (i, 0, j, 0))
    head_spec = pl.BlockSpec((1, MLA_HEADS, ts, LANES), lambda i, j: (i, 0, j, 0))
    pair_shape = jax.ShapeDtypeStruct((b, PAIRS, s, LANES), BF16)
    head_shape = jax.ShapeDtypeStruct((b, MLA_HEADS, s, LANES), BF16)

    def t_spec(n, rows):
        return pl.BlockSpec((1, n, rows, ts), lambda i, j: (i, 0, 0, j))

    def t_shape(n, rows):
        return jax.ShapeDtypeStruct((b, n, rows, s), BF16)

    return pl.pallas_call(
        _proj_body,
        grid=grid,
        in_specs=[
            pl.BlockSpec((1, ts, d), lambda i, j: (i, j, 0)),
            pl.BlockSpec((1, 1, ts), lambda i, j: (i, 0, j)),
            _full(gmix.shape), _full(wa.shape), _full(wb.shape), _full(bf.shape),
            _full(gq.shape), _full(wuq.shape), _full(gkv.shape), _full(wuk.shape),
            _full(wuv.shape), _full(freq.shape), _full(sign.shape),
        ],
        out_specs=[
            t_spec(PAIRS, LANES), pair_spec, t_spec(FOX_HEADS, VT_ROWS),
            pl.BlockSpec((1, FOX_HEADS, ts), lambda i, j: (i, 0, j)),
            t_spec(MLA_HEADS, LANES), head_spec, t_spec(MLA_HEADS, VT_ROWS),
        ],
        out_shape=[
            t_shape(PAIRS, LANES), pair_shape, t_shape(FOX_HEADS, VT_ROWS),
            jax.ShapeDtypeStruct((b, FOX_HEADS, s), F32),
            t_shape(MLA_HEADS, LANES), head_shape, t_shape(MLA_HEADS, VT_ROWS),
        ],
        scratch_shapes=[pltpu.VMEM((FOX_HEADS, 1), F32)],
        compiler_params=pltpu.CompilerParams(
            dimension_semantics=("parallel", "arbitrary"),
            vmem_limit_bytes=VMEM_LIMIT),
        name="proj",
    )(x, pos3, gmix, wa, wb, bf, gq, wuq, gkv, wuk, wuv, freq, sign)


def _attend(get_qt, get_k, get_vt, get_bias, o_ref, s_buf, seq):
    t = ATTN_TILE
    blk = MXU_TILE
    nb = t // blk
    key = lax.broadcasted_iota(jnp.int32, (blk, blk), 0)
    qry = lax.broadcasted_iota(jnp.int32, (blk, blk), 1)
    steps = [(i, j) for i in range(seq // t) for j in range(i + 1)]
    qts = {}
    acc = {}
    m_after = {}

    def visible(i, j, kb, qb):
        return j < i or kb <= qb

    def tile_offset(hh, i, j):
        return None if get_bias is None else get_bias(hh, i, j, 0)[1]

    def park(n, hh, qb):
        i, j = steps[n]
        if j == 0 and i not in qts:
            qts[i] = [get_qt(h2, pl.ds(i * t, t)) for h2 in range(2)]
        mx = jnp.full((1, blk), NEG_BIG, F32) if j == 0 else m_after[n - 1][hh][qb]
        for kb in range(nb):
            if not visible(i, j, kb, qb):
                continue
            s = _dot(get_k(hh, pl.ds(j * t + kb * blk, blk)),
                     qts[i][hh][:, qb * blk:(qb + 1) * blk])
            if get_bias is not None:
                key_part, _ = get_bias(hh, i, j, kb)
                s = s + jnp.concatenate([key_part] * (blk // LANES), axis=1)
            if j == i and kb == qb:
                s = jnp.where(key <= qry, s, NEG_BIG)
            s_buf[hh, n % 2, kb * blk:(kb + 1) * blk, qb * blk:(qb + 1) * blk] = s
            block_max = jnp.max(s, axis=0, keepdims=True)
            if get_bias is not None:
                block_max = block_max + tile_offset(hh, i, j)
            mx = jnp.maximum(mx, block_max)
        m_after.setdefault(n, [[None] * nb for _ in range(2)])[hh][qb] = mx

    def process(n, hh, qb):
        i, j = steps[n]
        n_keys = blk * sum(visible(i, j, kb, qb) for kb in range(nb))
        m_new = m_after[n][hh][qb]
        m_tile = m_new if get_bias is None else m_new - tile_offset(hh, i, j)
        p = jnp.exp2(s_buf[hh, n % 2, :n_keys, qb * blk:(qb + 1) * blk] - m_tile)
        pv = _dot(get_vt(hh, pl.ds(j * t, n_keys)), p.astype(BF16))
        if j == 0:
            acc.setdefault(i, [[None] * nb for _ in range(2)])[hh][qb] = pv
        else:
            acc[i][hh][qb] = jnp.exp2(m_after[n - 1][hh][qb] - m_new) * acc[i][hh][qb] + pv

    def finish(i):
        outs = [jnp.concatenate([a[:HALF] / a[HALF:HALF + 1] for a in acc[i][hh]], axis=1)
                for hh in range(2)]
        o_ref[0, 0, pl.ds(i * t, t), :] = jnp.concatenate(outs, axis=0).T.astype(o_ref.dtype)
        del acc[i], qts[i]

    parts = [(hh, qb) for qb in range(nb) for hh in range(2)]
    for hh, qb in parts:
        park(0, hh, qb)
    for n in range(len(steps)):
        for hh, qb in parts:
            if n + 1 < len(steps):
                park(n + 1, hh, qb)
            process(n, hh, qb)
        if steps[n][1] == steps[n][0]:
            finish(steps[n][0])


def _fox_attn_body(qt_ref, k_ref, vt_ref, c_ref, o_ref, s_buf):
    seq = k_ref.shape[2]
    t = ATTN_TILE
    feat = lax.broadcasted_iota(jnp.int32, (LANES, 1), 0)

    def get_qt(hh, cols):
        sel = (feat < HALF) if hh == 0 else (feat >= HALF)
        return jnp.where(sel, qt_ref[0, 0, :, cols], jnp.zeros((), BF16))

    c2 = c_ref[0, 0]
    r8 = lax.broadcasted_iota(jnp.int32, (8, seq), 0)
    c8 = jnp.where(r8 == 0, c2[0:1, :], jnp.where(r8 == 1, c2[1:2, :], 0.0))
    c_cols = jnp.concatenate([c8, jnp.zeros((LANES - 8, seq), F32)], axis=0).T

    def tile_start(hh, tile):
        return c_ref[0, 0, hh:hh + 1, tile * t:tile * t + 1]

    key_bias = [[jnp.broadcast_to((tile_start(hh, j) - c_cols[j * t:(j + 1) * t, hh:hh + 1])
                                  * LOG2E, (t, LANES))
                 for j in range(seq // t)] for hh in range(2)]

    def get_bias(hh, i, j, kb):
        rows = slice(kb * MXU_TILE, (kb + 1) * MXU_TILE)
        return key_bias[hh][j][rows], (tile_start(hh, i) - tile_start(hh, j)) * LOG2E

    _attend(get_qt, lambda hh, rows: k_ref[0, 0, rows, :],
            lambda hh, cols: vt_ref[0, hh, :, cols], get_bias, o_ref, s_buf, seq)


def _mla_attn_body(qt_ref, k_ref, vt_ref, o_ref, s_buf):
    seq = k_ref.shape[2]
    _attend(lambda hh, cols: qt_ref[0, hh, :, cols],
            lambda hh, rows: k_ref[0, hh, rows, :],
            lambda hh, cols: vt_ref[0, hh, :, cols], None, o_ref, s_buf, seq)


def _pair_block(s):
    return pl.BlockSpec((1, 1, s, LANES), lambda i, p: (i, p, 0, 0))


def _two_heads_block(s):
    return pl.BlockSpec((1, 2, s, LANES), lambda i, p: (i, p, 0, 0))


def _t_block(n, rows, s):
    return pl.BlockSpec((1, n, rows, s), lambda i, p: (i, p, 0, 0))


def _attn_scratch():
    return [pltpu.VMEM((2, 2, ATTN_TILE, ATTN_TILE), F32)]


def _fox_attention(fq, fk, fv, c4):
    b, _, s, _ = fk.shape
    return pl.pallas_call(
        _fox_attn_body,
        grid=(b, PAIRS),
        in_specs=[_t_block(1, LANES, s), _pair_block(s), _t_block(2, VT_ROWS, s),
                  pl.BlockSpec((1, 1, 2, s), lambda i, p: (i, p, 0, 0))],
        out_specs=_pair_block(s),
        out_shape=jax.ShapeDtypeStruct((b, PAIRS, s, LANES), BF16),
        scratch_shapes=_attn_scratch(),
        compiler_params=pltpu.CompilerParams(
            dimension_semantics=("parallel", "parallel"),
            vmem_limit_bytes=VMEM_LIMIT),
        name="fox_attn",
    )(fq, fk, fv, c4)


def _mla_attention(mq, mk, mv):
    b, _, s, _ = mk.shape
    return pl.pallas_call(
        _mla_attn_body,
        grid=(b, PAIRS),
        in_specs=[_t_block(2, LANES, s), _two_heads_block(s), _t_block(2, VT_ROWS, s)],
        out_specs=_pair_block(s),
        out_shape=jax.ShapeDtypeStruct((b, PAIRS, s, LANES), BF16),
        scratch_shapes=_attn_scratch(),
        compiler_params=pltpu.CompilerParams(
            dimension_semantics=("parallel", "parallel"),
            vmem_limit_bytes=VMEM_LIMIT),
        name="mla_attn",
    )(mq, mk, mv)


def _out_body(x_ref, fo_ref, mo_ref, gfox_ref, gmla_ref, wo_ref, gffn_ref,
              wg_ref, wu_ref, wd_ref, gfin_ref, o_ref):
    d_ff = wg_ref.shape[1]
    chunk = FFN_CHUNK_TILES * MXU_TILE

    def sub_tile(off):
        rows = pl.ds(off, FFN_SUB)
        fo = jnp.concatenate([fo_ref[0, p, rows, :] for p in range(PAIRS)], axis=-1).astype(F32)
        mo = jnp.concatenate([mo_ref[0, p, rows, :] for p in range(PAIRS)], axis=-1).astype(F32)
        mixed = jnp.concatenate([_rms(fo, gfox_ref[...]), _rms(mo, gmla_ref[...])],
                                axis=-1).astype(BF16)
        x1 = x_ref[0, rows, :] + _dot(mixed, wo_ref[...])
        h = _rms(x1, gffn_ref[...]).astype(BF16)
        yield
        z = None
        for c0 in range(0, d_ff, chunk):
            cols = slice(c0, min(c0 + chunk, d_ff))
            g = _dot(h, wg_ref[:, cols])
            u = _dot(h, wu_ref[:, cols])
            a = (g * (1.0 / (1.0 + jnp.exp(-g))) * u).astype(BF16)
            yield
            zc = _dot(a, wd_ref[cols, :])
            z = zc if z is None else z + zc
            yield
        o_ref[0, rows, :] = _rms(x1 + z, gfin_ref[...])
        yield

    tiles = [sub_tile(off) for off in range(0, x_ref.shape[1], FFN_SUB)]
    for _ in range(2 + 2 * len(range(0, d_ff, chunk))):
        for tile in tiles:
            next(tile)


def _const_spec(shape):
    return pl.BlockSpec(shape, lambda *_: (0,) * len(shape), pipeline_mode=pl.Buffered(1))


def _output_stage(x, fo, mo, gfox, gmla, wo, gffn, wg, wu, wd, gfin):
    b, s, d = x.shape
    ts = FFN_ROWS
    pair_spec = pl.BlockSpec((1, PAIRS, ts, LANES), lambda i, j: (i, 0, j, 0))
    tok_spec = pl.BlockSpec((1, ts, d), lambda i, j: (i, j, 0))
    return pl.pallas_call(
        _out_body,
        grid=(b, s // ts),
        in_specs=[tok_spec, pair_spec, pair_spec,
                  _const_spec(gfox.shape), _const_spec(gmla.shape), _const_spec(wo.shape),
                  _const_spec(gffn.shape), _const_spec(wg.shape), _const_spec(wu.shape),
                  _const_spec(wd.shape), _const_spec(gfin.shape)],
        out_specs=tok_spec,
        out_shape=jax.ShapeDtypeStruct((b, s, d), x.dtype),
        compiler_params=pltpu.CompilerParams(
            dimension_semantics=("parallel", "parallel"),
            vmem_limit_bytes=VMEM_LIMIT),
        name="out_ffn",
    )(x, fo, mo, gfox, gmla, wo, gffn, wg, wu, wd, gfin)


def _pad_heads(w, n_heads, width):
    k = w.shape[0]
    w = w.reshape(k, n_heads, width)
    w = jnp.pad(w, ((0, 0), (0, 0), (0, LANES - width)))
    return w.reshape(k, n_heads * LANES)


def kernel(x, positions, norm_mix_g, w_in, b_fgate, q_norm_g, w_uq, kv_norm_g, w_ukv,
           fox_out_g, mla_out_g, w_o, norm_ffn_g, w_gate, w_up, w_down, final_norm_g):
    b, s, d = x.shape
    assert norm_mix_g.shape[0] == 1, "single-layer block"
    assert s % ATTN_TILE == 0 and s % PROJ_ROWS == 0 and s % FFN_ROWS == 0

    w_in0 = w_in[0]
    o_f = 3 * FOX_WIDTH
    o_q = o_f + FOX_HEADS
    o_kv = o_q + MLA_Q_RANK
    o_kr = o_kv + MLA_KV_RANK
    wa = w_in0[:, :o_f].astype(BF16)
    misc = jnp.zeros((d, LANES), w_in0.dtype)
    misc = misc.at[:, :FOX_HEADS].set(w_in0[:, o_f:o_q])
    misc = misc.at[:, MLA_NOPE_DIM:MLA_QK_DIM].set(w_in0[:, o_kr:])
    wb = jnp.concatenate([w_in0[:, o_q:o_kr], misc], axis=1).astype(BF16)

    wuq = _pad_heads(w_uq[0], MLA_HEADS, MLA_QK_DIM).astype(BF16)
    w_ukv0 = w_ukv[0].reshape(MLA_KV_RANK, MLA_HEADS, MLA_NOPE_DIM + MLA_V_DIM)
    wuk = _pad_heads(w_ukv0[:, :, :MLA_NOPE_DIM].reshape(MLA_KV_RANK, -1),
                     MLA_HEADS, MLA_NOPE_DIM).astype(BF16)
    wuv = w_ukv0[:, :, MLA_NOPE_DIM:].reshape(MLA_KV_RANK, MLA_WIDTH).astype(BF16)

    half = MLA_ROPE_DIM // 2
    inv_freq = ROPE_THETA ** (-jnp.arange(0, MLA_ROPE_DIM, 2, dtype=F32) / MLA_ROPE_DIM)
    freq = jnp.concatenate([inv_freq, inv_freq]).reshape(MLA_ROPE_DIM, 1)
    sign = jnp.concatenate([-jnp.ones((half,), F32), jnp.ones((half,), F32)]).reshape(MLA_ROPE_DIM, 1)

    fq, fk, fv, c, mq, mk, mv = _projections(
        x, positions.reshape(b, 1, s), norm_mix_g, wa, wb, b_fgate.reshape(FOX_HEADS, 1),
        q_norm_g, wuq, kv_norm_g, wuk, wuv, freq, sign)

    fo = _fox_attention(fq, fk, fv, c.reshape(b, PAIRS, 2, s))
    mo = _mla_attention(mq, mk, mv)

    return _output_stage(
        x, fo, mo, fox_out_g, mla_out_g, w_o[0].astype(BF16), norm_ffn_g,
        w_gate[0].astype(BF16), w_up[0].astype(BF16), w_down[0].astype(BF16),
        final_norm_g.reshape(1, d))
```

```python
import functools

import jax
import jax.numpy as jnp
from jax import lax
from jax.experimental import pallas as pl
from jax.experimental.pallas import tpu as pltpu

D_MODEL = 1024
FOX_HEADS = 8
FOX_HEAD_DIM = 64
FOX_WIDTH = FOX_HEADS * FOX_HEAD_DIM
MLA_HEADS = 8
MLA_NOPE_DIM = 64
MLA_ROPE_DIM = 32
MLA_QK_DIM = MLA_NOPE_DIM + MLA_ROPE_DIM
MLA_V_DIM = 64
MLA_Q_RANK = 256
MLA_KV_RANK = 128
MLA_WIDTH = MLA_HEADS * MLA_V_DIM
ROPE_THETA = 10000.0
NORM_EPS = 1e-6

LANES = 128
HALF = LANES // 2
FOX_K_LANES = 2 * LANES
BIAS_PARTS = 3
VT_ROWS = HALF + 16
PAIRS = FOX_HEADS // 2
PROJ_ROWS = 1024
PROJ_SUB = 512
ATTN_TILE = 512
ATTN_PAIRS_PER_STEP = 2
FFN_ROWS = 512
FFN_SUB = 256
MXU_TILE = 256
FFN_CHUNK_TILES = 6
VMEM_LIMIT = 56 * 1024 * 1024
NEG_BIG = -1e30
LOG2E = 1.4426950408889634

F32 = jnp.float32
BF16 = jnp.bfloat16


def _rms(x, g):
    return x * lax.rsqrt(jnp.mean(x * x, axis=-1, keepdims=True) + NORM_EPS) * g


def _log_sigmoid(x):
    return jnp.minimum(x, 0.0) - jnp.log1p(jnp.exp(-jnp.abs(x)))


def _dot(a, b):
    return jnp.dot(a, b, preferred_element_type=F32)


def _proj_body(x_ref, pos_ref, gmix_ref, wa_ref, wb_ref, bf_ref, gq_ref, wuq_ref,
               gkv_ref, wuk_ref, wuv_ref, freq_ref, sign_ref,
               fq_ref, fk_ref, fv_ref, c_ref, mq_ref, mk_ref, mv_ref, carry_ref):
    ts = x_ref.shape[1]
    sub = PROJ_SUB
    lane = lax.broadcasted_iota(jnp.int32, (1, LANES), 1)
    first_half = (lane >= MLA_NOPE_DIM) & (lane < MLA_NOPE_DIM + MLA_ROPE_DIM // 2)
    rope_lanes = (lane >= MLA_NOPE_DIM) & (lane < MLA_QK_DIM)
    row8 = lax.broadcasted_iota(jnp.int32, (8, sub), 0)
    row = lax.broadcasted_iota(jnp.int32, (sub, sub), 0)
    col = lax.broadcasted_iota(jnp.int32, (sub, sub), 1)
    upper = jnp.where(row <= col, 1.0, 0.0).astype(BF16)
    q_scale = (MLA_QK_DIM ** -0.5) * LOG2E
    pad_lo = MLA_NOPE_DIM
    pad_hi = LANES - MLA_QK_DIM

    @pl.when(pl.program_id(1) == 0)
    def _():
        carry_ref[...] = jnp.zeros_like(carry_ref)

    carry = [carry_ref[...]]

    def rope(t, c, s):
        swapped = jnp.where(first_half,
                            pltpu.roll(t, LANES - MLA_ROPE_DIM // 2, axis=1),
                            pltpu.roll(t, MLA_ROPE_DIM // 2, axis=1))
        return t * c + swapped * s

    tail_row = lax.broadcasted_iota(jnp.int32, (VT_ROWS - HALF, sub), 0)
    ones_row = jnp.where(tail_row == 0, 1.0, 0.0).astype(BF16)

    def store_values(v_all, out_ref, rows):
        vt = v_all.T.astype(BF16)
        for hd in range(FOX_HEADS):
            out_ref[0, hd, :HALF, rows] = vt[hd * HALF:(hd + 1) * HALF, :]
            out_ref[0, hd, HALF:, rows] = ones_row

    def sub_tile(off):
        rows = pl.ds(off, sub)
        h = _rms(x_ref[0, rows, :], gmix_ref[...]).astype(BF16)
        pb = _dot(h, wb_ref[...])
        q_lat = pb[:, :MLA_Q_RANK]
        kv_lat = pb[:, MLA_Q_RANK:MLA_Q_RANK + MLA_KV_RANK]
        misc = pb[:, MLA_Q_RANK + MLA_KV_RANK:]
        yield

        v_fox = _dot(h, wa_ref[:, 2 * FOX_WIDTH:3 * FOX_WIDTH])
        f_logit = misc.T[:FOX_HEADS, :] + bf_ref[...]
        log_f = _log_sigmoid(f_logit)
        hi = log_f.astype(BF16)
        r1 = log_f - hi.astype(F32)
        mid = r1.astype(BF16)
        lo = (r1 - mid.astype(F32)).astype(BF16)
        parts = jnp.concatenate([hi, mid, lo], axis=0)
        cs = _dot(parts, upper)
        c_tile = (cs[:FOX_HEADS] + cs[FOX_HEADS:2 * FOX_HEADS] + cs[2 * FOX_HEADS:]) + carry[0]
        c_ref[0, :, rows] = c_tile
        carry[0] = c_tile[:, sub - 1:sub]
        bias = (c_tile[:, 0:1] - c_tile) * LOG2E
        b_hi = bias.astype(BF16).astype(F32)
        b_r1 = bias - b_hi
        b_mid = b_r1.astype(BF16).astype(F32)
        b_lo = (b_r1 - b_mid).astype(BF16).astype(F32)
        bias_tiles = []
        for p in range(PAIRS):
            picked = jnp.zeros((8, sub), F32)
            for slot, (part, hd) in enumerate((part, hd) for hd in (2 * p, 2 * p + 1)
                                              for part in (b_hi, b_mid, b_lo)):
                picked = jnp.where(row8 == slot, part[hd:hd + 1, :], picked)
            bias_tiles.append(jnp.concatenate(
                [picked, jnp.zeros((LANES - 8, sub), F32)], axis=0).T.astype(BF16))
        ang = freq_ref[...] * pos_ref[0, :, rows].astype(F32)
        cos = jnp.concatenate([jnp.ones((pad_lo, sub), F32), jnp.cos(ang),
                               jnp.zeros((pad_hi, sub), F32)], axis=0).T
        sin_signed = jnp.concatenate([jnp.zeros((pad_lo, sub), F32),
                                      jnp.sin(ang) * sign_ref[...],
                                      jnp.zeros((pad_hi, sub), F32)], axis=0).T
        qn = _rms(q_lat, gq_ref[...]).astype(BF16)
        kvn = _rms(kv_lat, gkv_ref[...]).astype(BF16)
        yield

        q_all = _dot(qn, wuq_ref[...])
        k_all = _dot(kvn, wuk_ref[...])
        v_mla = _dot(kvn, wuv_ref[...])
        store_values(v_fox, fv_ref, rows)
        yield

        pq = _dot(h, wa_ref[:, :FOX_WIDTH])
        for p in range(PAIRS):
            fq_ref[0, p, :, rows] = (pq[:, p * LANES:(p + 1) * LANES]
                                     * ((FOX_HEAD_DIM ** -0.5) * LOG2E)).T.astype(BF16)
        cos_q = cos * q_scale
        sin_q = sin_signed * q_scale
        for hd in range(MLA_HEADS):
            mq_ref[0, hd, :, rows] = rope(q_all[:, hd * LANES:(hd + 1) * LANES],
                                          cos_q, sin_q).T.astype(BF16)
        k_pe = rope(jnp.where(rope_lanes, misc, 0.0), cos, sin_signed)
        for hd in range(MLA_HEADS):
            mk_ref[0, hd, rows, :] = (k_all[:, hd * LANES:(hd + 1) * LANES] + k_pe).astype(BF16)
        store_values(v_mla, mv_ref, rows)
        yield

        pk = _dot(h, wa_ref[:, FOX_WIDTH:2 * FOX_WIDTH])
        for p in range(PAIRS):
            fk_ref[0, p, rows, :LANES] = pk[:, p * LANES:(p + 1) * LANES].astype(BF16)
            fk_ref[0, p, rows, LANES:] = bias_tiles[p]
        yield

    tiles = [sub_tile(off) for off in range(0, ts, sub)]
    assert len(tiles) == 2
    for which in (0, 1, 0, 0, 1, 1, 0, 1, 0, 1):
        next(tiles[which])
    carry_ref[...] = carry[0]


def _full(shape):
    return pl.BlockSpec(shape, lambda *_: (0,) * len(shape))


def _projections(x, pos3, gmix, wa, wb, bf, gq, wuq, gkv, wuk, wuv, freq, sign):
    b, s, d = x.shape
    ts = PROJ_ROWS
    grid = (b, s // ts)
    pair_spec = pl.BlockSpec((1, PAIRS, ts, LANES), lambda i, j: (i, 0, j, 0))
    head_spec = pl.BlockSpec((1, MLA_HEADS, ts, LANES), lambda i, j: (i, 0, j, 0))
    pair_shape = jax.ShapeDtypeStruct((b, PAIRS, s, LANES), BF16)
    head_shape = jax.ShapeDtypeStruct((b, MLA_HEADS, s, LANES), BF16)

    def t_spec(n, rows):
        return pl.BlockSpec((1, n, rows, ts), lambda i, j: (i, 0, 0, j))

    def t_shape(n, rows):
        return jax.ShapeDtypeStruct((b, n, rows, s), BF16)

    return pl.pallas_call(
        _proj_body,
        grid=grid,
        in_specs=[
            pl.BlockSpec((1, ts, d), lambda i, j: (i, j, 0)),
            pl.BlockSpec((1, 1, ts), lambda i, j: (i, 0, j)),
            _full(gmix.shape), _full(wa.shape), _full(wb.shape), _full(bf.shape),
            _full(gq.shape), _full(wuq.shape), _full(gkv.shape), _full(wuk.shape),
            _full(wuv.shape), _full(freq.shape), _full(sign.shape),
        ],
        out_specs=[
            t_spec(PAIRS, LANES),
            pl.BlockSpec((1, PAIRS, ts, FOX_K_LANES), lambda i, j: (i, 0, j, 0)),
            t_spec(FOX_HEADS, VT_ROWS),
            pl.BlockSpec((1, FOX_HEADS, ts), lambda i, j: (i, 0, j)),
            t_spec(MLA_HEADS, LANES), head_spec, t_spec(MLA_HEADS, VT_ROWS),
        ],
        out_shape=[
            t_shape(PAIRS, LANES),
            jax.ShapeDtypeStruct((b, PAIRS, s, FOX_K_LANES), BF16),
            t_shape(FOX_HEADS, VT_ROWS),
            jax.ShapeDtypeStruct((b, FOX_HEADS, s), F32),
            t_shape(MLA_HEADS, LANES), head_shape, t_shape(MLA_HEADS, VT_ROWS),
        ],
        scratch_shapes=[pltpu.VMEM((FOX_HEADS, 1), F32)],
        compiler_params=pltpu.CompilerParams(
            dimension_semantics=("parallel", "arbitrary"),
            vmem_limit_bytes=VMEM_LIMIT),
        name="proj",
    )(x, pos3, gmix, wa, wb, bf, gq, wuq, gkv, wuk, wuv, freq, sign)


def _attend(get_qt, get_k, get_vt, tile_offset, o_ref, s_buf, seq):
    t = ATTN_TILE
    blk = MXU_TILE
    nb = t // blk
    key = lax.broadcasted_iota(jnp.int32, (blk, blk), 0)
    qry = lax.broadcasted_iota(jnp.int32, (blk, blk), 1)
    steps = [(pr, i, j) for pr in range(ATTN_PAIRS_PER_STEP)
             for i in range(seq // t) for j in range(i + 1)]
    qts = {}
    acc = {}
    m_after = {}

    def visible(i, j, kb, qb):
        return j < i or kb <= qb

    def park(n, hh, qb):
        pr, i, j = steps[n]
        if j == 0 and (pr, i) not in qts:
            qts[pr, i] = [get_qt(pr, h2, pl.ds(i * t, t)) for h2 in range(2)]
        n_keys = blk * sum(visible(i, j, kb, qb) for kb in range(nb))
        s = _dot(get_k(pr, hh, pl.ds(j * t, n_keys)),
                 qts[pr, i][hh][:, qb * blk:(qb + 1) * blk])
        if j == i:
            diag = jnp.where(key <= qry, s[n_keys - blk:], NEG_BIG)
            s = diag if n_keys == blk else jnp.concatenate([s[:n_keys - blk], diag], axis=0)
        s_buf[hh, n % 2, :n_keys, qb * blk:(qb + 1) * blk] = s
        tile_max = jnp.max(s, axis=0, keepdims=True)
        if tile_offset is not None:
            tile_max = tile_max + tile_offset(pr, hh, i, j)
        prev = jnp.full((1, blk), NEG_BIG, F32) if j == 0 else m_after[n - 1][hh][qb]
        m_after.setdefault(n, [[None] * nb for _ in range(2)])[hh][qb] = jnp.maximum(prev, tile_max)

    def process(n, hh, qb):
        pr, i, j = steps[n]
        n_keys = blk * sum(visible(i, j, kb, qb) for kb in range(nb))
        m_new = m_after[n][hh][qb]
        m_tile = m_new if tile_offset is None else m_new - tile_offset(pr, hh, i, j)
        p = jnp.exp2(s_buf[hh, n % 2, :n_keys, qb * blk:(qb + 1) * blk] - m_tile)
        pv = _dot(get_vt(pr, hh, pl.ds(j * t, n_keys)), p.astype(BF16))
        if j == 0:
            acc.setdefault((pr, i), [[None] * nb for _ in range(2)])[hh][qb] = pv
        else:
            acc[pr, i][hh][qb] = (jnp.exp2(m_after[n - 1][hh][qb] - m_new) * acc[pr, i][hh][qb]
                                  + pv)

    def finish(pr, i):
        outs = [jnp.concatenate([a[:HALF] / a[HALF:HALF + 1] for a in acc[pr, i][hh]], axis=1)
                for hh in range(2)]
        o_ref[0, pr, pl.ds(i * t, t), :] = jnp.concatenate(outs, axis=0).T.astype(o_ref.dtype)
        del acc[pr, i], qts[pr, i]

    parts = [(hh, qb) for qb in range(nb) for hh in range(2)]
    for hh, qb in parts:
        park(0, hh, qb)
    for n in range(len(steps)):
        for hh, qb in parts:
            if n + 1 < len(steps):
                park(n + 1, hh, qb)
            process(n, hh, qb)
        if steps[n][2] == steps[n][1]:
            finish(steps[n][0], steps[n][1])


def _fox_attn_body(qt_ref, k_ref, vt_ref, c_ref, o_ref, s_buf):
    seq = k_ref.shape[2]
    t = ATTN_TILE
    feat = lax.broadcasted_iota(jnp.int32, (LANES, 1), 0)

    def get_qt(pr, hh, cols):
        sel = (feat < HALF) if hh == 0 else (feat >= HALF)
        own = (feat >= BIAS_PARTS * hh) & (feat < BIAS_PARTS * (hh + 1))
        ones_rows = jnp.broadcast_to(jnp.where(own, 1.0, 0.0).astype(BF16), (LANES, cols.size))
        return jnp.concatenate(
            [jnp.where(sel, qt_ref[0, pr, :, cols], jnp.zeros((), BF16)), ones_rows], axis=0)

    def tile_offset(pr, hh, i, j):
        start = lambda tile: c_ref[0, pr, hh:hh + 1, tile * t:tile * t + 1]
        return (start(i) - start(j)) * LOG2E

    _attend(get_qt, lambda pr, hh, rows: k_ref[0, pr, rows, :],
            lambda pr, hh, cols: vt_ref[0, 2 * pr + hh, :, cols], tile_offset, o_ref, s_buf, seq)


def _mla_attn_body(qt_ref, k_ref, vt_ref, o_ref, s_buf):
    seq = k_ref.shape[2]
    _attend(lambda pr, hh, cols: qt_ref[0, 2 * pr + hh, :, cols],
            lambda pr, hh, rows: k_ref[0, 2 * pr + hh, rows, :],
            lambda pr, hh, cols: vt_ref[0, 2 * pr + hh, :, cols], None, o_ref, s_buf, seq)


def _seq_block(n, s, lanes=LANES):
    return pl.BlockSpec((1, n, s, lanes), lambda i, p: (i, p, 0, 0))


def _t_block(n, rows, s):
    return pl.BlockSpec((1, n, rows, s), lambda i, p: (i, p, 0, 0))


def _attn_scratch():
    return [pltpu.VMEM((2, 2, ATTN_TILE, ATTN_TILE), F32)]


def _fox_attention(fq, fk, fv, c4):
    b, _, s, _ = fk.shape
    pp = ATTN_PAIRS_PER_STEP
    return pl.pallas_call(
        _fox_attn_body,
        grid=(b, PAIRS // pp),
        in_specs=[_t_block(pp, LANES, s), _seq_block(pp, s, FOX_K_LANES),
                  _t_block(2 * pp, VT_ROWS, s),
                  pl.BlockSpec((1, pp, 2, s), lambda i, p: (i, p, 0, 0))],
        out_specs=_seq_block(pp, s),
        out_shape=jax.ShapeDtypeStruct((b, PAIRS, s, LANES), BF16),
        scratch_shapes=_attn_scratch(),
        compiler_params=pltpu.CompilerParams(
            dimension_semantics=("parallel", "parallel"),
            vmem_limit_bytes=VMEM_LIMIT),
        name="fox_attn",
    )(fq, fk, fv, c4)


def _mla_attention(mq, mk, mv):
    b, _, s, _ = mk.shape
    pp = ATTN_PAIRS_PER_STEP
    return pl.pallas_call(
        _mla_attn_body,
        grid=(b, PAIRS // pp),
        in_specs=[_t_block(2 * pp, LANES, s), _seq_block(2 * pp, s),
                  _t_block(2 * pp, VT_ROWS, s)],
        out_specs=_seq_block(pp, s),
        out_shape=jax.ShapeDtypeStruct((b, PAIRS, s, LANES), BF16),
        scratch_shapes=_attn_scratch(),
        compiler_params=pltpu.CompilerParams(
            dimension_semantics=("parallel", "parallel"),
            vmem_limit_bytes=VMEM_LIMIT),
        name="mla_attn",
    )(mq, mk, mv)


def _out_body(x_ref, fo_ref, mo_ref, gfox_ref, gmla_ref, wo_ref, gffn_ref,
              wg_ref, wu_ref, wd_ref, gfin_ref, o_ref):
    d_ff = wg_ref.shape[1]
    chunk = FFN_CHUNK_TILES * MXU_TILE

    def sub_tile(off):
        rows = pl.ds(off, FFN_SUB)
        fo = jnp.concatenate([fo_ref[0, p, rows, :] for p in range(PAIRS)], axis=-1).astype(F32)
        mo = jnp.concatenate([mo_ref[0, p, rows, :] for p in range(PAIRS)], axis=-1).astype(F32)
        mixed = jnp.concatenate([_rms(fo, gfox_ref[...]), _rms(mo, gmla_ref[...])],
                                axis=-1).astype(BF16)
        x1 = x_ref[0, rows, :] + _dot(mixed, wo_ref[...])
        h = _rms(x1, gffn_ref[...]).astype(BF16)
        yield
        z = None
        for c0 in range(0, d_ff, chunk):
            cols = slice(c0, min(c0 + chunk, d_ff))
            g = _dot(h, wg_ref[:, cols])
            u = _dot(h, wu_ref[:, cols])
            a = (g * (1.0 / (1.0 + jnp.exp(-g))) * u).astype(BF16)
            yield
            zc = _dot(a, wd_ref[cols, :])
            z = zc if z is None else z + zc
            yield
        o_ref[0, rows, :] = _rms(x1 + z, gfin_ref[...])
        yield

    tiles = [sub_tile(off) for off in range(0, x_ref.shape[1], FFN_SUB)]
    for _ in range(2 + 2 * len(range(0, d_ff, chunk))):
        for tile in tiles:
            next(tile)


def _const_spec(shape):
    return pl.BlockSpec(shape, lambda *_: (0,) * len(shape), pipeline_mode=pl.Buffered(1))


def _output_stage(x, fo, mo, gfox, gmla, wo, gffn, wg, wu, wd, gfin):
    b, s, d = x.shape
    ts = FFN_ROWS
    pair_spec = pl.BlockSpec((1, PAIRS, ts, LANES), lambda i, j: (i, 0, j, 0))
    tok_spec = pl.BlockSpec((1, ts, d), lambda i, j: (i, j, 0))
    return pl.pallas_call(
        _out_body,
        grid=(b, s // ts),
        in_specs=[tok_spec, pair_spec, pair_spec,
                  _const_spec(gfox.shape), _const_spec(gmla.shape), _const_spec(wo.shape),
                  _const_spec(gffn.shape), _const_spec(wg.shape), _const_spec(wu.shape),
                  _const_spec(wd.shape), _const_spec(gfin.shape)],
        out_specs=tok_spec,
        out_shape=jax.ShapeDtypeStruct((b, s, d), x.dtype),
        compiler_params=pltpu.CompilerParams(
            dimension_semantics=("parallel", "parallel"),
            vmem_limit_bytes=VMEM_LIMIT),
        name="out_ffn",
    )(x, fo, mo, gfox, gmla, wo, gffn, wg, wu, wd, gfin)


def _pad_heads(w, n_heads, width):
    k = w.shape[0]
    w = w.reshape(k, n_heads, width)
    w = jnp.pad(w, ((0, 0), (0, 0), (0, LANES - width)))
    return w.reshape(k, n_heads * LANES)


def kernel(x, positions, norm_mix_g, w_in, b_fgate, q_norm_g, w_uq, kv_norm_g, w_ukv,
           fox_out_g, mla_out_g, w_o, norm_ffn_g, w_gate, w_up, w_down, final_norm_g):
    b, s, d = x.shape
    assert norm_mix_g.shape[0] == 1, "single-layer block"
    assert s % ATTN_TILE == 0 and s % PROJ_ROWS == 0 and s % FFN_ROWS == 0
    assert PROJ_SUB == ATTN_TILE, "the decay bias is relative to the attention key tile's first key"

    w_in0 = w_in[0]
    o_f = 3 * FOX_WIDTH
    o_q = o_f + FOX_HEADS
    o_kv = o_q + MLA_Q_RANK
    o_kr = o_kv + MLA_KV_RANK
    wa = w_in0[:, :o_f].astype(BF16)
    misc = jnp.zeros((d, LANES), w_in0.dtype)
    misc = misc.at[:, :FOX_HEADS].set(w_in0[:, o_f:o_q])
    misc = misc.at[:, MLA_NOPE_DIM:MLA_QK_DIM].set(w_in0[:, o_kr:])
    wb = jnp.concatenate([w_in0[:, o_q:o_kr], misc], axis=1).astype(BF16)

    wuq = _pad_heads(w_uq[0], MLA_HEADS, MLA_QK_DIM).astype(BF16)
    w_ukv0 = w_ukv[0].reshape(MLA_KV_RANK, MLA_HEADS, MLA_NOPE_DIM + MLA_V_DIM)
    wuk = _pad_heads(w_ukv0[:, :, :MLA_NOPE_DIM].reshape(MLA_KV_RANK, -1),
                     MLA_HEADS, MLA_NOPE_DIM).astype(BF16)
    wuv = w_ukv0[:, :, MLA_NOPE_DIM:].reshape(MLA_KV_RANK, MLA_WIDTH).astype(BF16)

    half = MLA_ROPE_DIM // 2
    inv_freq = ROPE_THETA ** (-jnp.arange(0, MLA_ROPE_DIM, 2, dtype=F32) / MLA_ROPE_DIM)
    freq = jnp.concatenate([inv_freq, inv_freq]).reshape(MLA_ROPE_DIM, 1)
    sign = jnp.concatenate([-jnp.ones((half,), F32), jnp.ones((half,), F32)]).reshape(MLA_ROPE_DIM, 1)

    fq, fk, fv, c, mq, mk, mv = _projections(
        x, positions.reshape(b, 1, s), norm_mix_g, wa, wb, b_fgate.reshape(FOX_HEADS, 1),
        q_norm_g, wuq, kv_norm_g, wuk, wuv, freq, sign)

    fo = _fox_attention(fq, fk, fv, c.reshape(b, PAIRS, 2, s))
    mo = _mla_attention(mq, mk, mv)

    return _output_stage(
        x, fo, mo, fox_out_g, mla_out_g, w_o[0].astype(BF16), norm_ffn_g,
        w_gate[0].astype(BF16), w_up[0].astype(BF16), w_down[0].astype(BF16),
        final_norm_g.reshape(1, d))
```

```python
import jax
import jax.numpy as jnp
import numpy as np
from jax import lax
from jax.experimental import pallas as pl
from jax.experimental.pallas import tpu as pltpu

D_MODEL = 1024
FOX_HEADS = 8
FOX_HEAD_DIM = 64
FOX_WIDTH = FOX_HEADS * FOX_HEAD_DIM
MLA_HEADS = 8
MLA_NOPE_DIM = 64
MLA_ROPE_DIM = 32
MLA_QK_DIM = MLA_NOPE_DIM + MLA_ROPE_DIM
MLA_V_DIM = 64
MLA_Q_RANK = 256
MLA_KV_RANK = 128
MLA_WIDTH = MLA_HEADS * MLA_V_DIM
ROPE_THETA = 10000.0
NORM_EPS = 1e-6

LANES = 128
SUBLANES = 8
BF16_ROWS = 2 * SUBLANES
HALF = LANES // 2
FOX_K_LANES = 2 * LANES
BIAS_PARTS = 3
VT_ROWS = HALF + BF16_ROWS
PAIRS = FOX_HEADS // 2
PROJ_ROWS = 1024
PROJ_SUB = 512
ATTN_TILE = 512
ATTN_PAIRS_PER_STEP = 2
FFN_ROWS = 512
FFN_SUB = 256
MXU_TILE = 256
FFN_CHUNK_TILES = 6
VMEM_LIMIT = 56 * 1024 * 1024
NEG_BIG = -1e30
LOG2E = 1.4426950408889634

F32 = jnp.float32
BF16 = jnp.bfloat16


def _rms(x, g):
    return x * lax.rsqrt(jnp.mean(x * x, axis=-1, keepdims=True) + NORM_EPS) * g


def _log_sigmoid(x):
    return jnp.minimum(x, 0.0) - jnp.log1p(jnp.exp(-jnp.abs(x)))


def _dot(a, b):
    return jnp.dot(a, b, preferred_element_type=F32)


def _proj_body(x_ref, pos_ref, gmix_ref, wa_ref, wb_ref, bf_ref, gq_ref, wuq_ref,
               gkv_ref, wuk_ref, wuv_ref, freq_ref, sign_ref,
               fq_ref, fk_ref, fv_ref, c_ref, mq_ref, mk_ref, mv_ref, carry_ref):
    ts = x_ref.shape[1]
    sub = PROJ_SUB
    lane = lax.broadcasted_iota(jnp.int32, (1, LANES), 1)
    first_half = (lane >= MLA_NOPE_DIM) & (lane < MLA_NOPE_DIM + MLA_ROPE_DIM // 2)
    rope_lanes = (lane >= MLA_NOPE_DIM) & (lane < MLA_QK_DIM)
    row8 = lax.broadcasted_iota(jnp.int32, (SUBLANES, sub), 0)
    row = lax.broadcasted_iota(jnp.int32, (sub, sub), 0)
    col = lax.broadcasted_iota(jnp.int32, (sub, sub), 1)
    upper = jnp.where(row <= col, 1.0, 0.0).astype(BF16)
    q_scale = (MLA_QK_DIM ** -0.5) * LOG2E
    pad_lo = MLA_NOPE_DIM
    pad_hi = LANES - MLA_QK_DIM

    @pl.when(pl.program_id(1) == 0)
    def _():
        carry_ref[...] = jnp.zeros_like(carry_ref)

    carry = [carry_ref[...]]

    def rope(t, c, s):
        swapped = jnp.where(first_half,
                            pltpu.roll(t, LANES - MLA_ROPE_DIM // 2, axis=1),
                            pltpu.roll(t, MLA_ROPE_DIM // 2, axis=1))
        return t * c + swapped * s

    tail_row = lax.broadcasted_iota(jnp.int32, (VT_ROWS - HALF, sub), 0)
    ones_row = jnp.where(tail_row == 0, 1.0, 0.0).astype(BF16)

    def store_values(v_all, out_ref, rows):
        vt = v_all.T.astype(BF16)
        for hd in range(FOX_HEADS):
            out_ref[0, hd, :HALF, rows] = vt[hd * HALF:(hd + 1) * HALF, :]
            out_ref[0, hd, HALF:, rows] = ones_row

    def sub_tile(off):
        rows = pl.ds(off, sub)
        h = _rms(x_ref[0, rows, :], gmix_ref[...]).astype(BF16)
        pb = _dot(h, wb_ref[...])
        q_lat = pb[:, :MLA_Q_RANK]
        kv_lat = pb[:, MLA_Q_RANK:MLA_Q_RANK + MLA_KV_RANK]
        misc = pb[:, MLA_Q_RANK + MLA_KV_RANK:]
        yield

        v_fox = _dot(h, wa_ref[:, 2 * FOX_WIDTH:3 * FOX_WIDTH])
        f_logit = misc.T[:FOX_HEADS, :] + bf_ref[...]
        log_f = _log_sigmoid(f_logit)
        hi = log_f.astype(BF16)
        r1 = log_f - hi.astype(F32)
        mid = r1.astype(BF16)
        lo = (r1 - mid.astype(F32)).astype(BF16)
        parts = jnp.concatenate([hi, mid, lo], axis=0)
        cs = _dot(parts, upper)
        c_tile = (cs[:FOX_HEADS] + cs[FOX_HEADS:2 * FOX_HEADS] + cs[2 * FOX_HEADS:]) + carry[0]
        c_ref[0, :, rows] = c_tile
        carry[0] = c_tile[:, sub - 1:sub]
        bias = (c_tile[:, 0:1] - c_tile) * LOG2E
        b_hi = bias.astype(BF16).astype(F32)
        b_r1 = bias - b_hi
        b_mid = b_r1.astype(BF16).astype(F32)
        b_lo = (b_r1 - b_mid).astype(BF16).astype(F32)
        bias_tiles = []
        for p in range(PAIRS):
            picked = jnp.zeros((SUBLANES, sub), F32)
            for slot, (part, hd) in enumerate((part, hd) for hd in (2 * p, 2 * p + 1)
                                              for part in (b_hi, b_mid, b_lo)):
                picked = jnp.where(row8 == slot, part[hd:hd + 1, :], picked)
            bias_tiles.append(jnp.concatenate(
                [picked, jnp.zeros((LANES - SUBLANES, sub), F32)], axis=0).T.astype(BF16))
        ang = freq_ref[...] * pos_ref[0, :, rows].astype(F32)
        cos = jnp.concatenate([jnp.ones((pad_lo, sub), F32), jnp.cos(ang),
                               jnp.zeros((pad_hi, sub), F32)], axis=0).T
        sin_signed = jnp.concatenate([jnp.zeros((pad_lo, sub), F32),
                                      jnp.sin(ang) * sign_ref[...],
                                      jnp.zeros((pad_hi, sub), F32)], axis=0).T
        qn = _rms(q_lat, gq_ref[...]).astype(BF16)
        kvn = _rms(kv_lat, gkv_ref[...]).astype(BF16)
        yield

        q_all = _dot(qn, wuq_ref[...])
        k_all = _dot(kvn, wuk_ref[...])
        v_mla = _dot(kvn, wuv_ref[...])
        store_values(v_fox, fv_ref, rows)
        yield

        pq = _dot(h, wa_ref[:, :FOX_WIDTH])
        for p in range(PAIRS):
            fq_ref[0, p, :, rows] = (pq[:, p * LANES:(p + 1) * LANES]
                                     * ((FOX_HEAD_DIM ** -0.5) * LOG2E)).T.astype(BF16)
        cos_q = cos * q_scale
        sin_q = sin_signed * q_scale
        for hd in range(MLA_HEADS):
            mq_ref[0, hd, :, rows] = rope(q_all[:, hd * LANES:(hd + 1) * LANES],
                                          cos_q, sin_q).T.astype(BF16)
        k_pe = rope(jnp.where(rope_lanes, misc, 0.0), cos, sin_signed)
        for hd in range(MLA_HEADS):
            mk_ref[0, hd, rows, :] = (k_all[:, hd * LANES:(hd + 1) * LANES] + k_pe).astype(BF16)
        store_values(v_mla, mv_ref, rows)
        yield

        pk = _dot(h, wa_ref[:, FOX_WIDTH:2 * FOX_WIDTH])
        for p in range(PAIRS):
            fk_ref[0, p, rows, :LANES] = pk[:, p * LANES:(p + 1) * LANES].astype(BF16)
            fk_ref[0, p, rows, LANES:] = bias_tiles[p]
        yield

    tiles = [sub_tile(off) for off in range(0, ts, sub)]
    assert len(tiles) == 2
    for which in (0, 1, 0, 0, 1, 1, 0, 1, 0, 1):
        next(tiles[which])
    carry_ref[...] = carry[0]


def _full(shape):
    return pl.BlockSpec(shape, lambda *_: (0,) * len(shape))


def _projections(x, pos3, gmix, wa, wb, bf, gq, wuq, gkv, wuk, wuv, freq, sign):
    b, s, d = x.shape
    ts = PROJ_ROWS
    grid = (b, s // ts)
    pair_spec = pl.BlockSpec((1, PAIRS, ts, LANES), lambda i, j: (i, 0, j, 0))
    head_spec = pl.BlockSpec((1, MLA_HEADS, ts, LANES), lambda i, j: (i, 0, j, 0))
    pair_shape = jax.ShapeDtypeStruct((b, PAIRS, s, LANES), BF16)
    head_shape = jax.ShapeDtypeStruct((b, MLA_HEADS, s, LANES), BF16)

    def t_spec(n, rows):
        return pl.BlockSpec((1, n, rows, ts), lambda i, j: (i, 0, 0, j))

    def t_shape(n, rows):
        return jax.ShapeDtypeStruct((b, n, rows, s), BF16)

    return pl.pallas_call(
        _proj_body,
        grid=grid,
        in_specs=[
            pl.BlockSpec((1, ts, d), lambda i, j: (i, j, 0)),
            pl.BlockSpec((1, 1, ts), lambda i, j: (i, 0, j)),
            _full(gmix.shape), _full(wa.shape), _full(wb.shape), _full(bf.shape),
            _full(gq.shape), _full(wuq.shape), _full(gkv.shape), _full(wuk.shape),
            _full(wuv.shape), _full(freq.shape), _full(sign.shape),
        ],
        out_specs=[
            t_spec(PAIRS, LANES),
            pl.BlockSpec((1, PAIRS, ts, FOX_K_LANES), lambda i, j: (i, 0, j, 0)),
            t_spec(FOX_HEADS, VT_ROWS),
            pl.BlockSpec((1, FOX_HEADS, ts), lambda i, j: (i, 0, j)),
            t_spec(MLA_HEADS, LANES), head_spec, t_spec(MLA_HEADS, VT_ROWS),
        ],
        out_shape=[
            t_shape(PAIRS, LANES),
            jax.ShapeDtypeStruct((b, PAIRS, s, FOX_K_LANES), BF16),
            t_shape(FOX_HEADS, VT_ROWS),
            jax.ShapeDtypeStruct((b, FOX_HEADS, s), F32),
            t_shape(MLA_HEADS, LANES), head_shape, t_shape(MLA_HEADS, VT_ROWS),
        ],
        scratch_shapes=[pltpu.VMEM((FOX_HEADS, 1), F32)],
        compiler_params=pltpu.CompilerParams(
            dimension_semantics=("parallel", "arbitrary"),
            vmem_limit_bytes=VMEM_LIMIT),
        name="proj",
    )(x, pos3, gmix, wa, wb, bf, gq, wuq, gkv, wuk, wuv, freq, sign)


def _attend(get_qt, get_k, get_vt, tile_offset, o_ref, s_buf, seq):
    t = ATTN_TILE
    blk = MXU_TILE
    nb = t // blk
    key = lax.broadcasted_iota(jnp.int32, (blk, blk), 0)
    qry = lax.broadcasted_iota(jnp.int32, (blk, blk), 1)
    steps = [(pr, i, j) for pr in range(ATTN_PAIRS_PER_STEP)
             for i in range(seq // t) for j in range(i + 1)]
    qts = {}
    acc = {}
    m_after = {}

    def visible(i, j, kb, qb):
        return j < i or kb <= qb

    def park(n, hh, qb):
        pr, i, j = steps[n]
        if j == 0 and (pr, i) not in qts:
            qts[pr, i] = [get_qt(pr, h2, pl.ds(i * t, t)) for h2 in range(2)]
        n_keys = blk * sum(visible(i, j, kb, qb) for kb in range(nb))
        s = _dot(get_k(pr, hh, pl.ds(j * t, n_keys)),
                 qts[pr, i][hh][:, qb * blk:(qb + 1) * blk])
        if j == i:
            diag = jnp.where(key <= qry, s[n_keys - blk:], NEG_BIG)
            s = diag if n_keys == blk else jnp.concatenate([s[:n_keys - blk], diag], axis=0)
        s_buf[hh, n % 2, :n_keys, qb * blk:(qb + 1) * blk] = s
        tile_max = jnp.max(s, axis=0, keepdims=True)
        if tile_offset is not None:
            tile_max = tile_max + tile_offset(pr, hh, i, j)
        prev = jnp.full((1, blk), NEG_BIG, F32) if j == 0 else m_after[n - 1][hh][qb]
        m_after.setdefault(n, [[None] * nb for _ in range(2)])[hh][qb] = jnp.maximum(prev, tile_max)

    def process(n, hh, qb):
        pr, i, j = steps[n]
        n_keys = blk * sum(visible(i, j, kb, qb) for kb in range(nb))
        m_new = m_after[n][hh][qb]
        m_tile = m_new if tile_offset is None else m_new - tile_offset(pr, hh, i, j)
        p = jnp.exp2(s_buf[hh, n % 2, :n_keys, qb * blk:(qb + 1) * blk] - m_tile)
        pv = _dot(get_vt(pr, hh, pl.ds(j * t, n_keys)), p.astype(BF16))
        if j == 0:
            acc.setdefault((pr, i), [[None] * nb for _ in range(2)])[hh][qb] = pv
        else:
            acc[pr, i][hh][qb] = (jnp.exp2(m_after[n - 1][hh][qb] - m_new) * acc[pr, i][hh][qb]
                                  + pv)

    def finish(pr, i):
        outs = [jnp.concatenate([a[:HALF] / a[HALF:HALF + 1] for a in acc[pr, i][hh]], axis=1)
                for hh in range(2)]
        o_ref[0, pr, pl.ds(i * t, t), :] = jnp.concatenate(outs, axis=0).T.astype(o_ref.dtype)
        del acc[pr, i], qts[pr, i]

    parts = [(hh, qb) for qb in range(nb) for hh in range(2)]
    for hh, qb in parts:
        park(0, hh, qb)
    for n in range(len(steps)):
        for hh, qb in parts:
            if n + 1 < len(steps):
                park(n + 1, hh, qb)
            process(n, hh, qb)
        if steps[n][2] == steps[n][1]:
            finish(steps[n][0], steps[n][1])


def _fox_attn_body(qt_ref, k_ref, vt_ref, c_ref, o_ref, s_buf):
    seq = k_ref.shape[2]
    t = ATTN_TILE
    feat = lax.broadcasted_iota(jnp.int32, (LANES, 1), 0)

    def get_qt(pr, hh, cols):
        sel = (feat < HALF) if hh == 0 else (feat >= HALF)
        own = (feat >= BIAS_PARTS * hh) & (feat < BIAS_PARTS * (hh + 1))
        ones_rows = jnp.broadcast_to(jnp.where(own, 1.0, 0.0).astype(BF16), (LANES, cols.size))
        return jnp.concatenate(
            [jnp.where(sel, qt_ref[0, pr, :, cols], jnp.zeros((), BF16)), ones_rows], axis=0)

    def tile_offset(pr, hh, i, j):
        start = lambda tile: c_ref[0, pr, hh:hh + 1, tile * t:tile * t + 1]
        return (start(i) - start(j)) * LOG2E

    _attend(get_qt, lambda pr, hh, rows: k_ref[0, pr, rows, :],
            lambda pr, hh, cols: vt_ref[0, 2 * pr + hh, :, cols], tile_offset, o_ref, s_buf, seq)


def _mla_attn_body(qt_ref, k_ref, vt_ref, o_ref, s_buf):
    seq = k_ref.shape[2]
    _attend(lambda pr, hh, cols: qt_ref[0, 2 * pr + hh, :, cols],
            lambda pr, hh, rows: k_ref[0, 2 * pr + hh, rows, :],
            lambda pr, hh, cols: vt_ref[0, 2 * pr + hh, :, cols], None, o_ref, s_buf, seq)


def _seq_block(n, s, lanes=LANES):
    return pl.BlockSpec((1, n, s, lanes), lambda i, p: (i, p, 0, 0))


def _t_block(n, rows, s):
    return pl.BlockSpec((1, n, rows, s), lambda i, p: (i, p, 0, 0))


def _attn_scratch():
    return [pltpu.VMEM((2, 2, ATTN_TILE, ATTN_TILE), F32)]


def _fox_attention(fq, fk, fv, c4):
    b, _, s, _ = fk.shape
    pp = ATTN_PAIRS_PER_STEP
    return pl.pallas_call(
        _fox_attn_body,
        grid=(b, PAIRS // pp),
        in_specs=[_t_block(pp, LANES, s), _seq_block(pp, s, FOX_K_LANES),
                  _t_block(2 * pp, VT_ROWS, s),
                  pl.BlockSpec((1, pp, 2, s), lambda i, p: (i, p, 0, 0))],
        out_specs=_seq_block(pp, s),
        out_shape=jax.ShapeDtypeStruct((b, PAIRS, s, LANES), BF16),
        scratch_shapes=_attn_scratch(),
        compiler_params=pltpu.CompilerParams(
            dimension_semantics=("parallel", "parallel"),
            vmem_limit_bytes=VMEM_LIMIT),
        name="fox_attn",
    )(fq, fk, fv, c4)


def _mla_attention(mq, mk, mv):
    b, _, s, _ = mk.shape
    pp = ATTN_PAIRS_PER_STEP
    return pl.pallas_call(
        _mla_attn_body,
        grid=(b, PAIRS // pp),
        in_specs=[_t_block(2 * pp, LANES, s), _seq_block(2 * pp, s),
                  _t_block(2 * pp, VT_ROWS, s)],
        out_specs=_seq_block(pp, s),
        out_shape=jax.ShapeDtypeStruct((b, PAIRS, s, LANES), BF16),
        scratch_shapes=_attn_scratch(),
        compiler_params=pltpu.CompilerParams(
            dimension_semantics=("parallel", "parallel"),
            vmem_limit_bytes=VMEM_LIMIT),
        name="mla_attn",
    )(mq, mk, mv)


def _out_body(x_ref, fo_ref, mo_ref, gfox_ref, gmla_ref, wo_ref, gffn_ref,
              wg_ref, wu_ref, wd_ref, gfin_ref, o_ref):
    d_ff = wg_ref.shape[1]
    chunk = FFN_CHUNK_TILES * MXU_TILE

    def sub_tile(off):
        rows = pl.ds(off, FFN_SUB)
        fo = jnp.concatenate([fo_ref[0, p, rows, :] for p in range(PAIRS)], axis=-1).astype(F32)
        mo = jnp.concatenate([mo_ref[0, p, rows, :] for p in range(PAIRS)], axis=-1).astype(F32)
        mixed = jnp.concatenate([_rms(fo, gfox_ref[...]), _rms(mo, gmla_ref[...])],
                                axis=-1).astype(BF16)
        x1 = x_ref[0, rows, :] + _dot(mixed, wo_ref[...])
        h = _rms(x1, gffn_ref[...]).astype(BF16)
        yield
        z = None
        for c0 in range(0, d_ff, chunk):
            cols = slice(c0, min(c0 + chunk, d_ff))
            g = _dot(h, wg_ref[:, cols])
            u = _dot(h, wu_ref[:, cols])
            a = (g * (1.0 / (1.0 + jnp.exp(-g))) * u).astype(BF16)
            yield
            zc = _dot(a, wd_ref[cols, :])
            z = zc if z is None else z + zc
            yield
        o_ref[0, rows, :] = _rms(x1 + z, gfin_ref[...])
        yield

    tiles = [sub_tile(off) for off in range(0, x_ref.shape[1], FFN_SUB)]
    for _ in range(2 + 2 * len(range(0, d_ff, chunk))):
        for tile in tiles:
            next(tile)


def _const_spec(shape):
    return pl.BlockSpec(shape, lambda *_: (0,) * len(shape), pipeline_mode=pl.Buffered(1))


def _output_stage(x, fo, mo, gfox, gmla, wo, gffn, wg, wu, wd, gfin):
    b, s, d = x.shape
    ts = FFN_ROWS
    pair_spec = pl.BlockSpec((1, PAIRS, ts, LANES), lambda i, j: (i, 0, j, 0))
    tok_spec = pl.BlockSpec((1, ts, d), lambda i, j: (i, j, 0))
    return pl.pallas_call(
        _out_body,
        grid=(b, s // ts),
        in_specs=[tok_spec, pair_spec, pair_spec,
                  _const_spec(gfox.shape), _const_spec(gmla.shape), _const_spec(wo.shape),
                  _const_spec(gffn.shape), _const_spec(wg.shape), _const_spec(wu.shape),
                  _const_spec(wd.shape), _const_spec(gfin.shape)],
        out_specs=tok_spec,
        out_shape=jax.ShapeDtypeStruct((b, s, d), x.dtype),
        compiler_params=pltpu.CompilerParams(
            dimension_semantics=("parallel", "parallel"),
            vmem_limit_bytes=VMEM_LIMIT),
        name="out_ffn",
    )(x, fo, mo, gfox, gmla, wo, gffn, wg, wu, wd, gfin)


def _pad_heads(w, n_heads, width):
    k = w.shape[0]
    w = w.reshape(k, n_heads, width)
    w = jnp.pad(w, ((0, 0), (0, 0), (0, LANES - width)))
    return w.reshape(k, n_heads * LANES)


def kernel(x, positions, norm_mix_g, w_in, b_fgate, q_norm_g, w_uq, kv_norm_g, w_ukv,
           fox_out_g, mla_out_g, w_o, norm_ffn_g, w_gate, w_up, w_down, final_norm_g):
    b, s, d = x.shape
    assert norm_mix_g.shape[0] == 1, "single-layer block"
    assert s % ATTN_TILE == 0 and s % PROJ_ROWS == 0 and s % FFN_ROWS == 0
    assert PROJ_SUB == ATTN_TILE, "the decay bias is relative to the attention key tile's first key"
    assert 2 * BIAS_PARTS <= SUBLANES, "one pair's bias parts are staged in one vreg of sublanes"

    w_in0 = w_in[0]
    o_f = 3 * FOX_WIDTH
    o_q = o_f + FOX_HEADS
    o_kv = o_q + MLA_Q_RANK
    o_kr = o_kv + MLA_KV_RANK
    wa = w_in0[:, :o_f].astype(BF16)
    wb = jnp.concatenate([
        w_in0[:, o_q:o_kr], w_in0[:, o_f:o_q],
        jnp.zeros((d, MLA_NOPE_DIM - FOX_HEADS), w_in0.dtype), w_in0[:, o_kr:],
        jnp.zeros((d, LANES - MLA_QK_DIM), w_in0.dtype)], axis=1).astype(BF16)

    wuq = _pad_heads(w_uq[0], MLA_HEADS, MLA_QK_DIM).astype(BF16)
    w_ukv0 = w_ukv[0].reshape(MLA_KV_RANK, MLA_HEADS, MLA_NOPE_DIM + MLA_V_DIM)
    wuk = _pad_heads(w_ukv0[:, :, :MLA_NOPE_DIM].reshape(MLA_KV_RANK, -1),
                     MLA_HEADS, MLA_NOPE_DIM).astype(BF16)
    wuv = w_ukv0[:, :, MLA_NOPE_DIM:].reshape(MLA_KV_RANK, MLA_WIDTH).astype(BF16)

    half = MLA_ROPE_DIM // 2
    inv_freq = ROPE_THETA ** (-jnp.arange(0, MLA_ROPE_DIM, 2, dtype=F32) / MLA_ROPE_DIM)
    freq = jnp.concatenate([inv_freq, inv_freq]).reshape(MLA_ROPE_DIM, 1)
    sign = jnp.asarray(np.repeat(np.float32([-1.0, 1.0]), half).reshape(MLA_ROPE_DIM, 1))

    fq, fk, fv, c, mq, mk, mv = _projections(
        x, positions.reshape(b, 1, s), norm_mix_g, wa, wb, b_fgate.reshape(FOX_HEADS, 1),
        q_norm_g, wuq, kv_norm_g, wuk, wuv, freq, sign)

    fo = _fox_attention(fq, fk, fv, c.reshape(b, PAIRS, 2, s))
    mo = _mla_attention(mq, mk, mv)

    return _output_stage(
        x, fo, mo, fox_out_g, mla_out_g, w_o[0].astype(BF16), norm_ffn_g,
        w_gate[0].astype(BF16), w_up[0].astype(BF16), w_down[0].astype(BF16),
        final_norm_g.reshape(1, d))
```
